```python
import math
import jax, jax.numpy as jnp
from jax import lax
import numpy as np

D_MODEL = 2048
BATCH = 2
SEQ = 4096
DEPTH = 2

N_A_LAYERS = DEPTH // 2
N_B_LAYERS = DEPTH - N_A_LAYERS
D_FF = 5632
LRU_WIDTH = D_MODEL
LRU_HEADS = 8
LRU_BLOCK = LRU_WIDTH // LRU_HEADS
CONV_W = 4
LRU_C = 8.0
N_HEADS = 16
HEAD_DIM = D_MODEL // N_HEADS
MOBA_BLOCK = 256
MOBA_TOPK = 3
Q_CHUNK = 32
EPS = 1e-6

kernel_name = 'yoco_rglru_moba_macaron'


def _rms_norm(x, g):
    xf = x.astype(jnp.float32)
    y = xf * lax.rsqrt(jnp.mean(xf * xf, axis=-1, keepdims=True) + EPS)
    return (y * g.astype(jnp.float32)).astype(x.dtype)


def _swiglu(h, w_in, w_out):
    gate, up = jnp.split(h @ w_in, 2, axis=-1)
    return (jax.nn.silu(gate) * up) @ w_out


def _causal_depthwise_conv(x, w, b):
    seq = x.shape[1]
    xp = jnp.pad(x, ((0, 0), (CONV_W - 1, 0), (0, 0)))
    y = b
    for tap in range(CONV_W):
        y = y + xp[:, tap:tap + seq] * w[tap]
    return y


def _linear_scan_combine(left, right):
    a_l, u_l = left
    a_r, u_r = right
    return a_l * a_r, a_r * u_l + u_r


def _rglru_block(hn, w_in, conv_w, conv_b, gate_w, gate_b, lam, w_out):
    bsz, seq, _ = hn.shape
    xb, yb = jnp.split(hn @ w_in, 2, axis=-1)
    yb = jax.nn.gelu(yb, approximate=True)
    xb = _causal_depthwise_conv(xb, conv_w, conv_b)
    xh = xb.reshape(bsz, seq, LRU_HEADS, LRU_BLOCK)
    gates = jnp.einsum('bshi,ghij->gbshj', xh, gate_w).reshape(2, bsz, seq, LRU_WIDTH)
    gates = jax.nn.sigmoid(gates.astype(jnp.float32) + gate_b[:, None, None, :].astype(jnp.float32))
    r, i = gates[0], gates[1]
    log_a = -LRU_C * r * jax.nn.softplus(-lam.astype(jnp.float32))
    a = jnp.exp(log_a)
    mult = jnp.sqrt(-jnp.expm1(2.0 * log_a))
    u = xb.astype(jnp.float32) * i * mult
    _, hs = lax.associative_scan(_linear_scan_combine, (a, u), axis=1)
    return (hs.astype(hn.dtype) * yb) @ w_out


def _shared_kv(h, kv_norm_g, w_kv, k_norm_g):
    bsz, seq, _ = h.shape
    n_blocks = -(-seq // MOBA_BLOCK)
    pad = n_blocks * MOBA_BLOCK - seq
    kv = (_rms_norm(h, kv_norm_g) @ w_kv).reshape(bsz, seq, 2, N_HEADS, HEAD_DIM)
    k = _rms_norm(kv[:, :, 0], k_norm_g)
    v = kv[:, :, 1]

    def to_blocks(t):
        t = jnp.pad(t, ((0, 0), (0, pad), (0, 0), (0, 0)))
        return t.reshape(bsz, n_blocks, MOBA_BLOCK, N_HEADS, HEAD_DIM).transpose(0, 3, 1, 2, 4)

    kb, vb = to_blocks(k), to_blocks(v)
    k_mean = jnp.mean(kb.astype(jnp.float32), axis=3)
    return kb, vb, k_mean


def _moba_attention(q, kb, vb, k_mean):
    bsz, n_heads, seq, hd = q.shape
    n_blocks = kb.shape[2]
    k_sel = min(MOBA_TOPK, n_blocks)
    scale = hd ** -0.5
    b_idx = jnp.arange(bsz)[:, None, None, None]
    h_idx = jnp.arange(n_heads)[None, :, None, None]

    def chunk(c):
        start = c * Q_CHUNK
        own = start // MOBA_BLOCK
        qc = lax.dynamic_slice_in_dim(q, start, Q_CHUNK, axis=2)
        gate = jnp.einsum('bhqd,bhnd->bhqn', qc.astype(jnp.float32), k_mean)
        past = jnp.arange(n_blocks) < own
        gate = jnp.where(past, gate, -jnp.inf)
        _, sel = lax.top_k(gate, k_sel)
        valid = sel < own
        k_g = kb[b_idx, h_idx, sel]
        v_g = vb[b_idx, h_idx, sel]
        s_sel = jnp.einsum('bhqd,bhqnkd->bhqnk', qc, k_g).astype(jnp.float32) * scale
        s_sel = jnp.where(valid[..., None], s_sel, -jnp.inf)
        s_sel = s_sel.reshape(bsz, n_heads, Q_CHUNK, k_sel * MOBA_BLOCK)
        k_own = lax.dynamic_index_in_dim(kb, own, axis=2, keepdims=False)
        v_own = lax.dynamic_index_in_dim(vb, own, axis=2, keepdims=False)
        s_own = jnp.einsum('bhqd,bhkd->bhqk', qc, k_own).astype(jnp.float32) * scale
        q_pos = start + jnp.arange(Q_CHUNK)
        k_pos = own * MOBA_BLOCK + jnp.arange(MOBA_BLOCK)
        s_own = jnp.where(k_pos[None, :] <= q_pos[:, None], s_own, -jnp.inf)
        p = jax.nn.softmax(jnp.concatenate([s_sel, s_own], axis=-1), axis=-1).astype(q.dtype)
        p_sel = p[..., :k_sel * MOBA_BLOCK].reshape(bsz, n_heads, Q_CHUNK, k_sel, MOBA_BLOCK)
        p_own = p[..., k_sel * MOBA_BLOCK:]
        return (jnp.einsum('bhqnk,bhqnkd->bhqd', p_sel, v_g)
                + jnp.einsum('bhqk,bhkd->bhqd', p_own, v_own))

    out = lax.map(chunk, jnp.arange(seq // Q_CHUNK))
    return out.transpose(1, 2, 0, 3, 4).reshape(bsz, n_heads, seq, hd)


def _moba_block(hn, kb, vb, k_mean, w_q, q_g, w_o):
    bsz, seq, _ = hn.shape
    q = _rms_norm((hn @ w_q).reshape(bsz, seq, N_HEADS, HEAD_DIM), q_g).transpose(0, 2, 1, 3)
    o = _moba_attention(q, kb, vb, k_mean)
    return o.transpose(0, 2, 1, 3).reshape(bsz, seq, N_HEADS * HEAD_DIM) @ w_o


def setup_inputs(seed: int = 0) -> dict:
    key = jax.random.key(seed)
    ks = jax.random.split(key, 17)
    f32 = jnp.float32

    def nrm(k, shape, fan_in):
        return jax.random.normal(k, shape, f32) * (fan_in ** -0.5)

    u = jax.random.uniform(ks[9], (N_A_LAYERS, LRU_WIDTH), f32, minval=0.9, maxval=0.999)
    return {
        'x': jax.random.normal(ks[0], (BATCH, SEQ, D_MODEL), f32),
        'norm_g': 1.0 + 0.02 * jax.random.normal(ks[1], (DEPTH, 3, D_MODEL), f32),
        'ffn_w_in': nrm(ks[2], (DEPTH, 2, D_MODEL, 2 * D_FF), D_MODEL),
        'ffn_w_out': nrm(ks[3], (DEPTH, 2, D_FF, D_MODEL), D_FF),
        'lru_w_in': nrm(ks[4], (N_A_LAYERS, D_MODEL, 2 * LRU_WIDTH), D_MODEL),
        'lru_conv_w': nrm(ks[5], (N_A_LAYERS, CONV_W, LRU_WIDTH), CONV_W),
        'lru_conv_b': 0.01 * jax.random.normal(ks[6], (N_A_LAYERS, LRU_WIDTH), f32),
        'lru_gate_w': nrm(ks[7], (N_A_LAYERS, 2, LRU_HEADS, LRU_BLOCK, LRU_BLOCK), LRU_BLOCK),
        'lru_gate_b': 0.01 * jax.random.normal(ks[8], (N_A_LAYERS, 2, LRU_WIDTH), f32),
        'lru_lambda': jnp.log(u) - jnp.log1p(-u),
        'lru_w_out': nrm(ks[10], (N_A_LAYERS, LRU_WIDTH, D_MODEL), LRU_WIDTH),
        'kv_norm_g': 1.0 + 0.02 * jax.random.normal(ks[11], (D_MODEL,), f32),
        'w_kv': nrm(ks[12], (D_MODEL, 2 * N_HEADS * HEAD_DIM), D_MODEL),
        'k_norm_g': 1.0 + 0.02 * jax.random.normal(ks[13], (HEAD_DIM,), f32),
        'attn_w_q': nrm(ks[14], (N_B_LAYERS, D_MODEL, N_HEADS * HEAD_DIM), D_MODEL),
        'q_norm_g': 1.0 + 0.02 * jax.random.normal(ks[15], (N_B_LAYERS, HEAD_DIM), f32),
        'attn_w_o': nrm(ks[16], (N_B_LAYERS, N_HEADS * HEAD_DIM, D_MODEL), N_HEADS * HEAD_DIM),
    }


def reference(x, norm_g, ffn_w_in, ffn_w_out, lru_w_in, lru_conv_w, lru_conv_b, lru_gate_w,
              lru_gate_b, lru_lambda, lru_w_out, kv_norm_g, w_kv, k_norm_g, attn_w_q, q_norm_g,
              attn_w_o):
    h = x
    kb = vb = k_mean = None
    for layer in range(DEPTH):
        h = h + 0.5 * _swiglu(_rms_norm(h, norm_g[layer, 0]), ffn_w_in[layer, 0], ffn_w_out[layer, 0])
        if layer < N_A_LAYERS:
            a = layer
            h = h + _rglru_block(_rms_norm(h, norm_g[layer, 1]), lru_w_in[a], lru_conv_w[a],
                                 lru_conv_b[a], lru_gate_w[a], lru_gate_b[a], lru_lambda[a],
                                 lru_w_out[a])
        else:
            j = layer - N_A_LAYERS
            h = h + _moba_block(_rms_norm(h, norm_g[layer, 1]), kb, vb, k_mean,
                                attn_w_q[j], q_norm_g[j], attn_w_o[j])
        h = h + 0.5 * _swiglu(_rms_norm(h, norm_g[layer, 2]), ffn_w_in[layer, 1], ffn_w_out[layer, 1])
        if layer == N_A_LAYERS - 1:
            kb, vb, k_mean = _shared_kv(h, kv_norm_g, w_kv, k_norm_g)
    return h
```

```python
import functools

import jax
import jax.numpy as jnp
from jax import lax
from jax.experimental import pallas as pl
from jax.experimental.pallas import tpu as pltpu

D_MODEL = 2048
D_FF = 5632
LRU_HEADS = 8
LRU_BLOCK = 256
CONV_W = 4
LRU_C = 8.0
N_HEADS = 16
HEAD_DIM = 128
MOBA_BLOCK = 256
MOBA_TOPK = 3
EPS = 1e-6

F32 = jnp.float32
BF16 = jnp.bfloat16

SUBLANES = 8
LANES = 128
VMEM_LIMIT_BYTES = 56 * 1024 * 1024
MASK_VALUE = -1e30


def _params(n_axes):
    return pltpu.CompilerParams(
        dimension_semantics=("arbitrary",) * n_axes,
        vmem_limit_bytes=VMEM_LIMIT_BYTES,
    )


def _rms_norm(x, g):
    return x * lax.rsqrt(jnp.mean(x * x, axis=-1, keepdims=True) + EPS) * g


def _dot(a, b):
    return jnp.dot(a, b, preferred_element_type=F32)


def _dot_nt(a, b):
    return lax.dot_general(a, b, (((1,), (1,)), ((), ())), preferred_element_type=F32)


FFN_TM = 512
FFN_TF = 512


def _ffn_kernel(x_ref, g_ref, wg_ref, wu_ref, wo_ref, o_ref, xn_ref):
    f = pl.program_id(1)

    @pl.when(f == 0)
    def _():
        x = x_ref[...]
        xn_ref[...] = _rms_norm(x, g_ref[...]).astype(BF16)
        o_ref[...] = x

    xn = xn_ref[...]
    gate = _dot(xn, wg_ref[...])
    up = _dot(xn, wu_ref[...])
    act = (0.5 * gate) * jax.nn.sigmoid(gate) * up
    o_ref[...] += _dot(act.astype(BF16), wo_ref[...])


def _ffn(x2d, g, w_in, w_out):
    m = x2d.shape[0]
    n_f = D_FF // FFN_TF
    return pl.pallas_call(
        _ffn_kernel,
        grid=(m // FFN_TM, n_f),
        in_specs=[
            pl.BlockSpec((FFN_TM, D_MODEL), lambda i, f: (i, 0)),
            pl.BlockSpec((1, D_MODEL), lambda i, f: (0, 0)),
            pl.BlockSpec((D_MODEL, FFN_TF), lambda i, f: (0, f)),
            pl.BlockSpec((D_MODEL, FFN_TF), lambda i, f: (0, f + n_f)),
            pl.BlockSpec((FFN_TF, D_MODEL), lambda i, f: (f, 0)),
        ],
        out_specs=pl.BlockSpec((FFN_TM, D_MODEL), lambda i, f: (i, 0)),
        out_shape=jax.ShapeDtypeStruct((m, D_MODEL), F32),
        scratch_shapes=[pltpu.VMEM((FFN_TM, D_MODEL), BF16)],
        compiler_params=_params(2),
        name="ffn",
    )(x2d, g.reshape(1, D_MODEL), w_in, w_in, w_out)


LRU_T = 512


def _softplus(x):
    return jnp.maximum(x, 0.0) + jnp.log1p(jnp.exp(-jnp.abs(x)))


def _lru_kernel(h_ref, g_ref, wx_ref, wy_ref, cw_ref, cb_ref, gw_ref, gb_ref, lam_ref, wo_ref,
                o_ref, xn_ref, carry_ref, halo_ref, hs_ref):
    s = pl.program_id(1)
    hd = pl.program_id(2)
    t, c = LRU_T, LRU_BLOCK

    @pl.when(hd == 0)
    def _():
        x = h_ref[0]
        xn_ref[...] = _rms_norm(x, g_ref[...]).astype(BF16)
        o_ref[0] = x

    @pl.when(s == 0)
    def _():
        carry_ref[hd] = jnp.zeros((SUBLANES, c), F32)
        halo_ref[hd] = jnp.zeros((SUBLANES, c), F32)

    xn = xn_ref[...]
    xb = _dot(xn, wx_ref[...])
    yb = _dot(xn, wy_ref[...])

    ext = jnp.concatenate([halo_ref[hd], xb], axis=0)
    halo_ref[hd] = xb[t - SUBLANES:]
    cw = cw_ref[...]
    xc = jnp.broadcast_to(cb_ref[...], (t, c))
    for tap in range(CONV_W):
        back = CONV_W - 1 - tap
        shifted = ext if back == 0 else pltpu.roll(ext, back, 0)
        xc = xc + shifted[SUBLANES:] * cw[tap:tap + 1]

    xcb = xc.astype(BF16)
    gb = gb_ref[...]
    r = jax.nn.sigmoid(_dot(xcb, gw_ref[0, 0]) + gb[0:1])
    i = jax.nn.sigmoid(_dot(xcb, gw_ref[1, 0]) + gb[1:2])
    log_a = (-LRU_C * r) * _softplus(-lam_ref[...])
    a = jnp.exp(log_a)
    u = xc * i * jnp.sqrt(-jnp.tanh(log_a) * (a * a + 1.0))

    row_in_group = lax.broadcasted_iota(jnp.int32, (t, c), 0) & (SUBLANES - 1)
    d = 1
    while d < SUBLANES:
        keep = row_in_group >= d
        a_prev = jnp.where(keep, pltpu.roll(a, d, 0), 1.0)
        u_prev = jnp.where(keep, pltpu.roll(u, d, 0), 0.0)
        u = a * u_prev + u
        a = a * a_prev
        d *= 2
    prev = carry_ref[hd]
    for grp in range(t // SUBLANES):
        rows = slice(grp * SUBLANES, (grp + 1) * SUBLANES)
        hg = u[rows] + a[rows] * prev
        hs_ref[rows, :] = hg
        prev = jnp.broadcast_to(hg[SUBLANES - 1:SUBLANES], (SUBLANES, c))
    carry_ref[hd] = prev

    mixed = hs_ref[...] * jax.nn.gelu(yb, approximate=True)
    o_ref[0] += _dot(mixed.astype(BF16), wo_ref[...])


def _lru(h, g, w_in, conv_w, conv_b, gate_w, gate_b, lam, w_out):
    bsz, seq, _ = h.shape
    t, c = LRU_T, LRU_BLOCK
    return pl.pallas_call(
        _lru_kernel,
        grid=(bsz, seq // t, LRU_HEADS),
        in_specs=[
            pl.BlockSpec((1, t, D_MODEL), lambda b, s, k: (b, s, 0)),
            pl.BlockSpec((1, D_MODEL), lambda b, s, k: (0, 0)),
            pl.BlockSpec((D_MODEL, c), lambda b, s, k: (0, k)),
            pl.BlockSpec((D_MODEL, c), lambda b, s, k: (0, k + LRU_HEADS)),
            pl.BlockSpec((CONV_W, c), lambda b, s, k: (0, k)),
            pl.BlockSpec((1, c), lambda b, s, k: (0, k)),
            pl.BlockSpec((2, 1, c, c), lambda b, s, k: (0, k, 0, 0)),
            pl.BlockSpec((2, c), lambda b, s, k: (0, k)),
            pl.BlockSpec((1, c), lambda b, s, k: (0, k)),
            pl.BlockSpec((c, D_MODEL), lambda b, s, k: (k, 0)),
        ],
        out_specs=pl.BlockSpec((1, t, D_MODEL), lambda b, s, k: (b, s, 0)),
        out_shape=jax.ShapeDtypeStruct(h.shape, F32),
        scratch_shapes=[
            pltpu.VMEM((t, D_MODEL), BF16),
            pltpu.VMEM((LRU_HEADS, SUBLANES, c), F32),
            pltpu.VMEM((LRU_HEADS, SUBLANES, c), F32),
            pltpu.VMEM((t, c), F32),
        ],
        compiler_params=_params(3),
        name="lru",
    )(h, g.reshape(1, D_MODEL), w_in, w_in, conv_w, conv_b.reshape(1, -1), gate_w, gate_b,
      lam.reshape(1, -1), w_out)


KV_T = 512
HEAD_GROUP = 8
GROUP_COLS = HEAD_GROUP * HEAD_DIM
AUG_DIM = 2 * HEAD_DIM


def _head_rms_norm(x, g):
    return x * lax.rsqrt(jnp.mean(x * x, axis=-1, keepdims=True) + EPS) * g


def _kv_kernel(h_ref, g_ref, wk_ref, wv_ref, kg_ref, k_ref, v_ref, km_ref, xn_ref):
    s = pl.program_id(1)
    n = pl.program_id(2)
    t = KV_T

    @pl.when(n == 0)
    def _():
        xn_ref[...] = _rms_norm(h_ref[0], g_ref[...]).astype(BF16)

    xn = xn_ref[...]
    k = _dot(xn, wk_ref[...])
    v = _dot(xn, wv_ref[...])
    v_ref[0] = v.astype(BF16)

    row = lax.broadcasted_iota(jnp.int32, (t, LANES), 0)
    lane = lax.broadcasted_iota(jnp.int32, (t, LANES), 1)
    block_of_row = s * (t // MOBA_BLOCK) + row // MOBA_BLOCK
    onehot = jnp.where(lane == block_of_row, 1.0, 0.0).astype(BF16)
    for hl in range(HEAD_GROUP):
        kh = _head_rms_norm(k[:, hl * HEAD_DIM:(hl + 1) * HEAD_DIM], kg_ref[...])
        k_ref[0, :, hl * AUG_DIM:hl * AUG_DIM + HEAD_DIM] = kh.astype(BF16)
        k_ref[0, :, hl * AUG_DIM + HEAD_DIM:(hl + 1) * AUG_DIM] = onehot
        for blk in range(t // MOBA_BLOCK):
            km_ref[0, blk, :, hl * HEAD_DIM:(hl + 1) * HEAD_DIM] = jnp.mean(
                kh[blk * MOBA_BLOCK:(blk + 1) * MOBA_BLOCK], axis=0, keepdims=True)


def _shared_kv(h, g, w_kv, k_g):
    bsz, seq, _ = h.shape
    t = KV_T
    n_groups = N_HEADS // HEAD_GROUP
    n_blocks = seq // MOBA_BLOCK
    return pl.pallas_call(
        _kv_kernel,
        grid=(bsz, seq // t, n_groups),
        in_specs=[
            pl.BlockSpec((1, t, D_MODEL), lambda b, s, n: (b, s, 0)),
            pl.BlockSpec((1, D_MODEL), lambda b, s, n: (0, 0)),
            pl.BlockSpec((D_MODEL, GROUP_COLS), lambda b, s, n: (0, n)),
            pl.BlockSpec((D_MODEL, GROUP_COLS), lambda b, s, n: (0, n + n_groups)),
            pl.BlockSpec((1, HEAD_DIM), lambda b, s, n: (0, 0)),
        ],
        out_specs=[
            pl.BlockSpec((1, t, HEAD_GROUP * AUG_DIM), lambda b, s, n: (b, s, n)),
            pl.BlockSpec((1, t, GROUP_COLS), lambda b, s, n: (b, s, n)),
            pl.BlockSpec((1, t // MOBA_BLOCK, 1, GROUP_COLS), lambda b, s, n: (b, s, 0, n)),
        ],
        out_shape=[
            jax.ShapeDtypeStruct((bsz, seq, N_HEADS * AUG_DIM), BF16),
            jax.ShapeDtypeStruct((bsz, seq, N_HEADS * HEAD_DIM), BF16),
            jax.ShapeDtypeStruct((bsz, n_blocks, 1, N_HEADS * HEAD_DIM), F32),
        ],
        scratch_shapes=[pltpu.VMEM((t, D_MODEL), BF16)],
        compiler_params=_params(3),
        name="shared_kv",
    )(h, g.reshape(1, D_MODEL), w_kv, w_kv, k_g.reshape(1, HEAD_DIM))


Q_T = 512
GATE_SLOTS = LANES // HEAD_GROUP


def _split_bf16(x):
    hi = x.astype(BF16)
    lo = (x - hi.astype(F32)).astype(BF16)
    return hi, lo


def _q_kernel(h_ref, g_ref, wq_ref, qg_ref, km_ref, q_ref, xn_ref):
    s = pl.program_id(1)
    n = pl.program_id(2)
    t = Q_T

    @pl.when(n == 0)
    def _():
        xn_ref[...] = _rms_norm(h_ref[0], g_ref[...]).astype(BF16)

    q = _dot(xn_ref[...], wq_ref[...])
    qn = jnp.concatenate(
        [_head_rms_norm(q[:, hl * HEAD_DIM:(hl + 1) * HEAD_DIM], qg_ref[...])
         for hl in range(HEAD_GROUP)], axis=1)

    km = km_ref[0]
    km_rows = jnp.concatenate([km] * HEAD_GROUP, axis=0)
    r_idx = lax.broadcasted_iota(jnp.int32, (LANES, GROUP_COLS), 0)
    c_idx = lax.broadcasted_iota(jnp.int32, (LANES, GROUP_COLS), 1)
    km_diag = jnp.where(r_idx // GATE_SLOTS == c_idx // HEAD_DIM, km_rows, 0.0)
    q_hi, q_lo = _split_bf16(qn)
    k_hi, k_lo = _split_bf16(km_diag)
    gate = _dot_nt(q_hi, k_hi) + (_dot_nt(q_hi, k_lo) + _dot_nt(q_lo, k_hi))

    lane = lax.broadcasted_iota(jnp.int32, (t, LANES), 1)
    row = lax.broadcasted_iota(jnp.int32, (t, LANES), 0)
    blk = lane & (GATE_SLOTS - 1)
    own = s * (t // MOBA_BLOCK) + row // MOBA_BLOCK
    gate = jnp.where(blk < own, gate, -jnp.inf)
    rank = jnp.zeros((t, LANES), jnp.int32)
    for sh in range(1, GATE_SLOTS):
        lower = pltpu.roll(gate, sh, 1)
        rank += jnp.where((blk >= sh) & (lower >= gate), 1, 0)
        upper = pltpu.roll(gate, LANES - sh, 1)
        rank += jnp.where((blk < GATE_SLOTS - sh) & (upper > gate), 1, 0)
    attend = ((rank < MOBA_TOPK) & (blk < own)) | (blk >= own)
    bias = jnp.where(attend, 0.0, MASK_VALUE)

    for hl in range(HEAD_GROUP):
        q_ref[0, :, hl * AUG_DIM:hl * AUG_DIM + HEAD_DIM] = (
            qn[:, hl * HEAD_DIM:(hl + 1) * HEAD_DIM].astype(BF16))
        shift = (LANES - hl * GATE_SLOTS) % LANES
        bias_h = bias if shift == 0 else pltpu.roll(bias, shift, 1)
        q_ref[0, :, hl * AUG_DIM + HEAD_DIM:(hl + 1) * AUG_DIM] = (
            jnp.where(lane < GATE_SLOTS, bias_h, 0.0).astype(BF16))


def _queries(h, g, w_q, q_g, k_mean):
    bsz, seq, _ = h.shape
    t = Q_T
    n_groups = N_HEADS // HEAD_GROUP
    n_blocks = seq // MOBA_BLOCK
    assert n_blocks == GATE_SLOTS
    return pl.pallas_call(
        _q_kernel,
        grid=(bsz, seq // t, n_groups),
        in_specs=[
            pl.BlockSpec((1, t, D_MODEL), lambda b, s, n: (b, s, 0)),
            pl.BlockSpec((1, D_MODEL), lambda b, s, n: (0, 0)),
            pl.BlockSpec((D_MODEL, GROUP_COLS), lambda b, s, n: (0, n)),
            pl.BlockSpec((1, HEAD_DIM), lambda b, s, n: (0, 0)),
            pl.BlockSpec((1, n_blocks, GROUP_COLS), lambda b, s, n: (b, 0, n)),
        ],
        out_specs=pl.BlockSpec((1, t, HEAD_GROUP * AUG_DIM), lambda b, s, n: (b, s, n)),
        out_shape=jax.ShapeDtypeStruct((bsz, seq, N_HEADS * AUG_DIM), BF16),
        scratch_shapes=[pltpu.VMEM((t, D_MODEL), BF16)],
        compiler_params=_params(3),
        name="queries",
    )(h, g.reshape(1, D_MODEL), w_q, q_g.reshape(1, HEAD_DIM),
      k_mean.reshape(bsz, n_blocks, N_HEADS * HEAD_DIM))


def _attn_kernel(q_ref, k_ref, v_ref, o_ref):
    i = pl.program_id(2)
    blk = MOBA_BLOCK
    scale = HEAD_DIM ** -0.5
    q = q_ref[0]

    def scores(j):
        start = pl.multiple_of(j * blk, blk)
        return _dot_nt(q, k_ref[0, pl.ds(start, blk), :]) * scale, v_ref[0, pl.ds(start, blk), :]

    s_own, v_own = scores(i)
    q_pos = lax.broadcasted_iota(jnp.int32, (blk, blk), 0)
    k_pos = lax.broadcasted_iota(jnp.int32, (blk, blk), 1)
    s_own = jnp.where(k_pos <= q_pos, s_own, MASK_VALUE)
    m0 = jnp.max(s_own, axis=-1, keepdims=True)
    p0 = jnp.exp(s_own - m0)
    l0 = jnp.sum(p0, axis=-1, keepdims=True)
    acc0 = _dot(p0.astype(BF16), v_own)

    def body(j, carry):
        m, l, acc = carry
        s_j, v_j = scores(j)
        m_new = jnp.maximum(m, jnp.max(s_j, axis=-1, keepdims=True))
        alpha = jnp.exp(m - m_new)
        p = jnp.exp(s_j - m_new)
        l = alpha * l + jnp.sum(p, axis=-1, keepdims=True)
        acc = alpha * acc + _dot(p.astype(BF16), v_j)
        return m_new, l, acc

    _, l, acc = lax.fori_loop(0, i, body, (m0, l0, acc0))
    o_ref[0] = (acc / l).astype(BF16)


def _attention(q_aug, k_aug, v):
    bsz, seq, _ = v.shape
    blk = MOBA_BLOCK
    return pl.pallas_call(
        _attn_kernel,
        grid=(bsz, N_HEADS, seq // blk),
        in_specs=[
            pl.BlockSpec((1, blk, AUG_DIM), lambda b, h, i: (b, i, h)),
            pl.BlockSpec((1, seq, AUG_DIM), lambda b, h, i: (b, 0, h)),
            pl.BlockSpec((1, seq, HEAD_DIM), lambda b, h, i: (b, 0, h)),
        ],
        out_specs=pl.BlockSpec((1, blk, HEAD_DIM), lambda b, h, i: (b, i, h)),
        out_shape=jax.ShapeDtypeStruct((bsz, seq, N_HEADS * HEAD_DIM), BF16),
        compiler_params=_params(3),
        name="moba_attention",
    )(q_aug, k_aug, v)


PROJ_TM = 1024
PROJ_TN = 512


def _out_proj_kernel(o_ref, w_ref, x_ref, y_ref):
    y_ref[...] = x_ref[...] + _dot(o_ref[...], w_ref[...])


def _out_proj(o2d, w_o, x2d):
    m, k = o2d.shape
    n = w_o.shape[1]
    return pl.pallas_call(
        _out_proj_kernel,
        grid=(m // PROJ_TM, n // PROJ_TN),
        in_specs=[
            pl.BlockSpec((PROJ_TM, k), lambda i, j: (i, 0)),
            pl.BlockSpec((k, PROJ_TN), lambda i, j: (0, j)),
            pl.BlockSpec((PROJ_TM, PROJ_TN), lambda i, j: (i, j)),
        ],
        out_specs=pl.BlockSpec((PROJ_TM, PROJ_TN), lambda i, j: (i, j)),
        out_shape=jax.ShapeDtypeStruct((m, n), F32),
        compiler_params=_params(2),
        name="out_proj",
    )(o2d, w_o, x2d)


def kernel(x, norm_g, ffn_w_in, ffn_w_out, lru_w_in, lru_conv_w, lru_conv_b, lru_gate_w, lru_gate_b, lru_lambda, lru_w_out, kv_norm_g, w_kv, k_norm_g, attn_w_q, q_norm_g, attn_w_o):
    bsz, seq, d = x.shape
    rows = bsz * seq
    ffn_w_in = ffn_w_in.astype(BF16)
    ffn_w_out = ffn_w_out.astype(BF16)

    def ffn(h, layer, which):
        return _ffn(h.reshape(rows, d), norm_g[layer, 2 * which], ffn_w_in[layer, which],
                    ffn_w_out[layer, which]).reshape(bsz, seq, d)

    h = ffn(x, 0, 0)
    h = _lru(h, norm_g[0, 1], lru_w_in[0].astype(BF16), lru_conv_w[0], lru_conv_b[0],
             lru_gate_w[0].astype(BF16), lru_gate_b[0], lru_lambda[0], lru_w_out[0].astype(BF16))
    h = ffn(h, 0, 1)
    k_aug, v, k_mean = _shared_kv(h, kv_norm_g, w_kv.astype(BF16), k_norm_g)

    h = ffn(h, 1, 0)
    q_aug = _queries(h, norm_g[1, 1], attn_w_q[0].astype(BF16), q_norm_g[0], k_mean)
    o = _attention(q_aug, k_aug, v)
    h = _out_proj(o.reshape(rows, d), attn_w_o[0].astype(BF16), h.reshape(rows, d)).reshape(bsz, seq, d)
    h = ffn(h, 1, 1)
    return h
```

```python
import functools

import jax
import jax.numpy as jnp
from jax import lax
from jax.experimental import pallas as pl
from jax.experimental.pallas import tpu as pltpu

D_MODEL = 2048
D_FF = 5632
LRU_HEADS = 8
LRU_BLOCK = 256
CONV_W = 4
LRU_C = 8.0
N_HEADS = 16
HEAD_DIM = 128
MOBA_BLOCK = 256
MOBA_TOPK = 3
EPS = 1e-6

F32 = jnp.float32
BF16 = jnp.bfloat16

SUBLANES = 8
LANES = 128
VMEM_LIMIT_BYTES = 56 * 1024 * 1024
MASK_VALUE = -1e30
LOG2_E = 1.4426950408889634


def _params(n_axes):
    return pltpu.CompilerParams(
        dimension_semantics=("arbitrary",) * n_axes,
        vmem_limit_bytes=VMEM_LIMIT_BYTES,
    )


def _rms_norm(x, g):
    return x * lax.rsqrt(jnp.mean(x * x, axis=-1, keepdims=True) + EPS) * g


def _dot(a, b):
    return jnp.dot(a, b, preferred_element_type=F32)


def _dot_nt(a, b):
    return lax.dot_general(a, b, (((1,), (1,)), ((), ())), preferred_element_type=F32)


FFN_TM = 512
FFN_TF = 512


def _ffn_kernel(x_ref, g_ref, wg_ref, wu_ref, wo_ref, o_ref, xn_ref):
    f = pl.program_id(1)

    @pl.when(f == 0)
    def _():
        x = x_ref[...]
        xn_ref[...] = _rms_norm(x, g_ref[...]).astype(BF16)
        o_ref[...] = x

    xn = xn_ref[...]
    gate = _dot(xn, wg_ref[...])
    up = _dot(xn, wu_ref[...])
    act = (0.5 * gate) * jax.nn.sigmoid(gate) * up
    o_ref[...] += _dot(act.astype(BF16), wo_ref[...])


def _ffn(x2d, g, w_in, w_out, layer, which):
    m = x2d.shape[0]
    n_f = D_FF // FFN_TF
    return pl.pallas_call(
        _ffn_kernel,
        grid=(m // FFN_TM, n_f),
        in_specs=[
            pl.BlockSpec((FFN_TM, D_MODEL), lambda i, f: (i, 0)),
            pl.BlockSpec((1, D_MODEL), lambda i, f: (0, 0)),
            pl.BlockSpec((None, None, D_MODEL, FFN_TF), lambda i, f: (layer, which, 0, f)),
            pl.BlockSpec((None, None, D_MODEL, FFN_TF), lambda i, f: (layer, which, 0, f + n_f)),
            pl.BlockSpec((None, None, FFN_TF, D_MODEL), lambda i, f: (layer, which, f, 0)),
        ],
        out_specs=pl.BlockSpec((FFN_TM, D_MODEL), lambda i, f: (i, 0)),
        out_shape=jax.ShapeDtypeStruct((m, D_MODEL), F32),
        scratch_shapes=[pltpu.VMEM((FFN_TM, D_MODEL), BF16)],
        compiler_params=_params(2),
        name="ffn",
    )(x2d, g.reshape(1, D_MODEL), w_in, w_in, w_out)


LRU_T = 512


def _softplus(x):
    return jnp.maximum(x, 0.0) + jnp.log1p(jnp.exp(-jnp.abs(x)))


def _lru_kernel(h_ref, g_ref, wx_ref, wy_ref, cw_ref, cb_ref, gw_ref, gb_ref, lam_ref, wo_ref,
                o_ref, xn_ref, carry_ref, halo_ref, hs_ref):
    s = pl.program_id(1)
    hd = pl.program_id(2)
    t, c = LRU_T, LRU_BLOCK

    @pl.when(hd == 0)
    def _():
        x = h_ref[0]
        xn_ref[...] = _rms_norm(x, g_ref[...]).astype(BF16)
        o_ref[0] = x

    @pl.when(s == 0)
    def _():
        carry_ref[hd] = jnp.zeros((SUBLANES, c), F32)
        halo_ref[hd] = jnp.zeros((SUBLANES, c), F32)

    xn = xn_ref[...]
    xb = _dot(xn, wx_ref[...])
    yb = _dot(xn, wy_ref[...])

    ext = jnp.concatenate([halo_ref[hd], xb], axis=0)
    halo_ref[hd] = xb[t - SUBLANES:]
    cw = cw_ref[...]
    xc = jnp.broadcast_to(cb_ref[...], (t, c))
    for tap in range(CONV_W):
        back = CONV_W - 1 - tap
        shifted = ext if back == 0 else pltpu.roll(ext, back, 0)
        xc = xc + shifted[SUBLANES:] * cw[tap:tap + 1]

    xcb = xc.astype(BF16)
    gb = gb_ref[...]
    r = jax.nn.sigmoid(_dot(xcb, gw_ref[0, 0]) + gb[0:1])
    i = jax.nn.sigmoid(_dot(xcb, gw_ref[1, 0]) + gb[1:2])
    log_a = (-LRU_C * r) * _softplus(-lam_ref[...])
    a = jnp.exp(log_a)
    u = xc * i * jnp.sqrt(-jnp.tanh(log_a) * (a * a + 1.0))

    row_in_group = lax.broadcasted_iota(jnp.int32, (t, c), 0) & (SUBLANES - 1)
    d = 1
    while d < SUBLANES:
        keep = row_in_group >= d
        a_prev = jnp.where(keep, pltpu.roll(a, d, 0), 1.0)
        u_prev = jnp.where(keep, pltpu.roll(u, d, 0), 0.0)
        u = a * u_prev + u
        a = a * a_prev
        d *= 2
    prev = carry_ref[hd]
    for grp in range(t // SUBLANES):
        rows = slice(grp * SUBLANES, (grp + 1) * SUBLANES)
        hg = u[rows] + a[rows] * prev
        hs_ref[rows, :] = hg
        prev = jnp.broadcast_to(hg[SUBLANES - 1:SUBLANES], (SUBLANES, c))
    carry_ref[hd] = prev

    mixed = hs_ref[...] * jax.nn.gelu(yb, approximate=True)
    o_ref[0] += _dot(mixed.astype(BF16), wo_ref[...])


def _lru(h, g, w_in, conv_w, conv_b, gate_w, gate_b, lam, w_out):
    bsz, seq, _ = h.shape
    t, c = LRU_T, LRU_BLOCK
    return pl.pallas_call(
        _lru_kernel,
        grid=(bsz, seq // t, LRU_HEADS),
        in_specs=[
            pl.BlockSpec((1, t, D_MODEL), lambda b, s, k: (b, s, 0)),
            pl.BlockSpec((1, D_MODEL), lambda b, s, k: (0, 0)),
            pl.BlockSpec((D_MODEL, c), lambda b, s, k: (0, k)),
            pl.BlockSpec((D_MODEL, c), lambda b, s, k: (0, k + LRU_HEADS)),
            pl.BlockSpec((CONV_W, c), lambda b, s, k: (0, k)),
            pl.BlockSpec((1, c), lambda b, s, k: (0, k)),
            pl.BlockSpec((2, 1, c, c), lambda b, s, k: (0, k, 0, 0)),
            pl.BlockSpec((2, c), lambda b, s, k: (0, k)),
            pl.BlockSpec((1, c), lambda b, s, k: (0, k)),
            pl.BlockSpec((c, D_MODEL), lambda b, s, k: (k, 0)),
        ],
        out_specs=pl.BlockSpec((1, t, D_MODEL), lambda b, s, k: (b, s, 0)),
        out_shape=jax.ShapeDtypeStruct(h.shape, F32),
        scratch_shapes=[
            pltpu.VMEM((t, D_MODEL), BF16),
            pltpu.VMEM((LRU_HEADS, SUBLANES, c), F32),
            pltpu.VMEM((LRU_HEADS, SUBLANES, c), F32),
            pltpu.VMEM((t, c), F32),
        ],
        compiler_params=_params(3),
        name="lru",
    )(h, g.reshape(1, D_MODEL), w_in, w_in, conv_w, conv_b.reshape(1, -1), gate_w, gate_b,
      lam.reshape(1, -1), w_out)


KV_T = 512
HEAD_GROUP = 8
GROUP_COLS = HEAD_GROUP * HEAD_DIM
AUG_DIM = 2 * HEAD_DIM


def _head_rms_norm(x, g):
    return x * lax.rsqrt(jnp.mean(x * x, axis=-1, keepdims=True) + EPS) * g


def _kv_kernel(h_ref, g_ref, wk_ref, wv_ref, kg_ref, k_ref, vt_ref, km_ref, xn_ref):
    s = pl.program_id(1)
    n = pl.program_id(2)
    t = KV_T

    @pl.when(n == 0)
    def _():
        xn_ref[...] = _rms_norm(h_ref[0], g_ref[...]).astype(BF16)

    xn = xn_ref[...]
    k = _dot(xn, wk_ref[...])
    v = _dot(xn, wv_ref[...])
    vt_ref[0] = v.T.astype(BF16)

    row = lax.broadcasted_iota(jnp.int32, (t, LANES), 0)
    lane = lax.broadcasted_iota(jnp.int32, (t, LANES), 1)
    block_of_row = s * (t // MOBA_BLOCK) + row // MOBA_BLOCK
    onehot = jnp.where(lane == block_of_row, 1.0, 0.0).astype(BF16)
    for hl in range(HEAD_GROUP):
        kh = _head_rms_norm(k[:, hl * HEAD_DIM:(hl + 1) * HEAD_DIM], kg_ref[...])
        k_ref[0, :, hl * AUG_DIM:hl * AUG_DIM + HEAD_DIM] = kh.astype(BF16)
        k_ref[0, :, hl * AUG_DIM + HEAD_DIM:(hl + 1) * AUG_DIM] = onehot
        for blk in range(t // MOBA_BLOCK):
            km_ref[0, blk, :, hl * HEAD_DIM:(hl + 1) * HEAD_DIM] = jnp.mean(
                kh[blk * MOBA_BLOCK:(blk + 1) * MOBA_BLOCK], axis=0, keepdims=True)


def _shared_kv(h, g, w_kv, k_g):
    bsz, seq, _ = h.shape
    t = KV_T
    n_groups = N_HEADS // HEAD_GROUP
    n_blocks = seq // MOBA_BLOCK
    return pl.pallas_call(
        _kv_kernel,
        grid=(bsz, seq // t, n_groups),
        in_specs=[
            pl.BlockSpec((1, t, D_MODEL), lambda b, s, n: (b, s, 0)),
            pl.BlockSpec((1, D_MODEL), lambda b, s, n: (0, 0)),
            pl.BlockSpec((D_MODEL, GROUP_COLS), lambda b, s, n: (0, n)),
            pl.BlockSpec((D_MODEL, GROUP_COLS), lambda b, s, n: (0, n + n_groups)),
            pl.BlockSpec((1, HEAD_DIM), lambda b, s, n: (0, 0)),
        ],
        out_specs=[
            pl.BlockSpec((1, t, HEAD_GROUP * AUG_DIM), lambda b, s, n: (b, s, n)),
            pl.BlockSpec((1, GROUP_COLS, t), lambda b, s, n: (b, n, s)),
            pl.BlockSpec((1, t // MOBA_BLOCK, 1, GROUP_COLS), lambda b, s, n: (b, s, 0, n)),
        ],
        out_shape=[
            jax.ShapeDtypeStruct((bsz, seq, N_HEADS * AUG_DIM), BF16),
            jax.ShapeDtypeStruct((bsz, N_HEADS * HEAD_DIM, seq), BF16),
            jax.ShapeDtypeStruct((bsz, n_blocks, 1, N_HEADS * HEAD_DIM), F32),
        ],
        scratch_shapes=[pltpu.VMEM((t, D_MODEL), BF16)],
        compiler_params=_params(3),
        name="shared_kv",
    )(h, g.reshape(1, D_MODEL), w_kv, w_kv, k_g.reshape(1, HEAD_DIM))


Q_T = 512
GATE_SLOTS = SUBLANES * 2


def _split_bf16(x):
    hi = x.astype(BF16)
    lo = (x - hi.astype(F32)).astype(BF16)
    return hi, lo


def _q_kernel(h_ref, g_ref, wqt_ref, qg_ref, km_ref, qt_ref, xn_ref):
    s = pl.program_id(1)
    n = pl.program_id(2)
    t = Q_T

    @pl.when(n == 0)
    def _():
        xn_ref[...] = _rms_norm(h_ref[0], g_ref[...]).astype(BF16)

    q_t = _dot_nt(wqt_ref[...], xn_ref[...])
    qg = jnp.broadcast_to(qg_ref[...], (HEAD_DIM, t))
    heads = []
    for hl in range(HEAD_GROUP):
        x = q_t[hl * HEAD_DIM:(hl + 1) * HEAD_DIM]
        heads.append(x * lax.rsqrt(jnp.mean(x * x, axis=0, keepdims=True) + EPS) * qg)
    qn_t = jnp.concatenate(heads, axis=0)

    km = km_ref[0]
    km_rows = jnp.concatenate([km] * HEAD_GROUP, axis=0)
    r_idx = lax.broadcasted_iota(jnp.int32, (HEAD_GROUP * GATE_SLOTS, GROUP_COLS), 0)
    c_idx = lax.broadcasted_iota(jnp.int32, (HEAD_GROUP * GATE_SLOTS, GROUP_COLS), 1)
    km_diag = jnp.where(r_idx // GATE_SLOTS == c_idx // HEAD_DIM, km_rows, 0.0)
    q_hi, q_lo = _split_bf16(qn_t)
    k_hi, k_lo = _split_bf16(km_diag)
    gate = _dot(k_hi, q_hi) + (_dot(k_lo, q_hi) + _dot(k_hi, q_lo))

    blk = lax.broadcasted_iota(jnp.int32, (GATE_SLOTS, t), 0)
    col = lax.broadcasted_iota(jnp.int32, (GATE_SLOTS, t), 1)
    own = s * (t // MOBA_BLOCK) + col // MOBA_BLOCK
    past = blk < own
    zeros_tail = jnp.zeros((AUG_DIM - HEAD_DIM - GATE_SLOTS, t), BF16)
    for hl in range(HEAD_GROUP):
        g = jnp.where(past, gate[hl * GATE_SLOTS:(hl + 1) * GATE_SLOTS], -jnp.inf)
        rank = jnp.zeros((GATE_SLOTS, t), jnp.int32)
        for other in range(GATE_SLOTS):
            go = jnp.broadcast_to(g[other:other + 1], (GATE_SLOTS, t))
            beats = (go > g) | ((go == g) & (blk > other))
            rank += jnp.where(beats, 1, 0)
        attend = ((rank < MOBA_TOPK) & past) | (blk >= own)
        bias = jnp.where(attend, 0.0, MASK_VALUE)
        base = hl * AUG_DIM
        qt_ref[0, base:base + HEAD_DIM, :] = heads[hl].astype(BF16)
        qt_ref[0, base + HEAD_DIM:base + HEAD_DIM + GATE_SLOTS, :] = bias.astype(BF16)
        qt_ref[0, base + HEAD_DIM + GATE_SLOTS:base + AUG_DIM, :] = zeros_tail


def _queries(h, g, w_q_t, q_g, k_mean):
    bsz, seq, _ = h.shape
    t = Q_T
    n_groups = N_HEADS // HEAD_GROUP
    n_blocks = seq // MOBA_BLOCK
    assert n_blocks == GATE_SLOTS
    return pl.pallas_call(
        _q_kernel,
        grid=(bsz, seq // t, n_groups),
        in_specs=[
            pl.BlockSpec((1, t, D_MODEL), lambda b, s, n: (b, s, 0)),
            pl.BlockSpec((1, D_MODEL), lambda b, s, n: (0, 0)),
            pl.BlockSpec((GROUP_COLS, D_MODEL), lambda b, s, n: (n, 0)),
            pl.BlockSpec((HEAD_DIM, 1), lambda b, s, n: (0, 0)),
            pl.BlockSpec((1, n_blocks, GROUP_COLS), lambda b, s, n: (b, 0, n)),
        ],
        out_specs=pl.BlockSpec((1, HEAD_GROUP * AUG_DIM, t), lambda b, s, n: (b, n, s)),
        out_shape=jax.ShapeDtypeStruct((bsz, N_HEADS * AUG_DIM, seq), BF16),
        scratch_shapes=[pltpu.VMEM((t, D_MODEL), BF16)],
        compiler_params=_params(3),
        name="queries",
    )(h, g.reshape(1, D_MODEL), w_q_t, q_g.reshape(HEAD_DIM, 1),
      k_mean.reshape(bsz, n_blocks, N_HEADS * HEAD_DIM))


def _attn_kernel(qt_ref, k_ref, vt_ref, o_ref, m_ref, l_ref, acc_ref):
    blk = MOBA_BLOCK
    n_blocks = qt_ref.shape[2] // blk
    scale = HEAD_DIM ** -0.5 * LOG2_E
    k_pos = lax.broadcasted_iota(jnp.int32, (blk, blk), 0)
    q_pos = lax.broadcasted_iota(jnp.int32, (blk, blk), 1)
    causal = k_pos <= q_pos

    def rows(j):
        return slice(j * blk, (j + 1) * blk)

    own = [_dot(k_ref[0, rows(i), :], qt_ref[0, :, rows(i)]) for i in range(n_blocks)]
    for i in range(n_blocks):
        s = jnp.where(causal, own[i] * scale, MASK_VALUE)
        m = jnp.max(s, axis=0, keepdims=True)
        p = jnp.exp2(s - m)
        m_ref[:, rows(i)] = m
        l_ref[:, rows(i)] = jnp.sum(p, axis=0, keepdims=True)
        acc_ref[:, rows(i)] = _dot(vt_ref[0, :, rows(i)], p.astype(BF16))

    for j in range(n_blocks - 1):
        later = slice((j + 1) * blk, n_blocks * blk)
        s = _dot(k_ref[0, rows(j), :], qt_ref[0, :, later]) * scale
        m_old = m_ref[:, later]
        m_new = jnp.maximum(m_old, jnp.max(s, axis=0, keepdims=True))
        alpha = jnp.exp2(m_old - m_new)
        p = jnp.exp2(s - m_new)
        m_ref[:, later] = m_new
        l_ref[:, later] = alpha * l_ref[:, later] + jnp.sum(p, axis=0, keepdims=True)
        acc_ref[:, later] = alpha * acc_ref[:, later] + _dot(vt_ref[0, :, rows(j)], p.astype(BF16))

    o = acc_ref[...] * (1.0 / l_ref[...])
    o_ref[0] = o.T.astype(BF16)


def _attention(q_aug, k_aug, v_t):
    bsz, _, seq = v_t.shape
    n_blocks = seq // MOBA_BLOCK
    return pl.pallas_call(
        _attn_kernel,
        grid=(bsz, N_HEADS),
        in_specs=[
            pl.BlockSpec((1, AUG_DIM, seq), lambda b, h: (b, h, 0)),
            pl.BlockSpec((1, seq, AUG_DIM), lambda b, h: (b, 0, h)),
            pl.BlockSpec((1, HEAD_DIM, seq), lambda b, h: (b, h, 0)),
        ],
        out_specs=pl.BlockSpec((1, seq, HEAD_DIM), lambda b, h: (b, 0, h)),
        out_shape=jax.ShapeDtypeStruct((bsz, seq, N_HEADS * HEAD_DIM), BF16),
        scratch_shapes=[
            pltpu.VMEM((1, seq), F32),
            pltpu.VMEM((1, seq), F32),
            pltpu.VMEM((HEAD_DIM, seq), F32),
        ],
        compiler_params=_params(2),
        name="moba_attention",
    )(q_aug, k_aug, v_t)


PROJ_TM = 1024
PROJ_TN = 512


def _out_proj_kernel(o_ref, w_ref, x_ref, y_ref):
    y_ref[...] = x_ref[...] + _dot(o_ref[...], w_ref[...])


def _out_proj(o2d, w_o, x2d):
    m, k = o2d.shape
    n = w_o.shape[1]
    return pl.pallas_call(
        _out_proj_kernel,
        grid=(m // PROJ_TM, n // PROJ_TN),
        in_specs=[
            pl.BlockSpec((PROJ_TM, k), lambda i, j: (i, 0)),
            pl.BlockSpec((k, PROJ_TN), lambda i, j: (0, j)),
            pl.BlockSpec((PROJ_TM, PROJ_TN), lambda i, j: (i, j)),
        ],
        out_specs=pl.BlockSpec((PROJ_TM, PROJ_TN), lambda i, j: (i, j)),
        out_shape=jax.ShapeDtypeStruct((m, n), F32),
        compiler_params=_params(2),
        name="out_proj",
    )(o2d, w_o, x2d)


def kernel(x, norm_g, ffn_w_in, ffn_w_out, lru_w_in, lru_conv_w, lru_conv_b, lru_gate_w, lru_gate_b, lru_lambda, lru_w_out, kv_norm_g, w_kv, k_norm_g, attn_w_q, q_norm_g, attn_w_o):
    bsz, seq, d = x.shape
    rows = bsz * seq
    ffn_w_in = ffn_w_in.astype(BF16)
    ffn_w_out = ffn_w_out.astype(BF16)

    def ffn(h, layer, which):
        return _ffn(h.reshape(rows, d), norm_g[layer, 2 * which], ffn_w_in, ffn_w_out,
                    layer, which).reshape(bsz, seq, d)

    h = ffn(x, 0, 0)
    h = _lru(h, norm_g[0, 1], lru_w_in[0].astype(BF16), lru_conv_w[0], lru_conv_b[0],
             lru_gate_w[0].astype(BF16), lru_gate_b[0], lru_lambda[0], lru_w_out[0].astype(BF16))
    h = ffn(h, 0, 1)
    k_aug, v_t, k_mean = _shared_kv(h, kv_norm_g, w_kv.astype(BF16), k_norm_g)

    h = ffn(h, 1, 0)
    q_aug_t = _queries(h, norm_g[1, 1], attn_w_q[0].T.astype(BF16), q_norm_g[0], k_mean)
    o = _attention(q_aug_t, k_aug, v_t)
    h = _out_proj(o.reshape(rows, d), attn_w_o[0].astype(BF16), h.reshape(rows, d)).reshape(bsz, seq, d)
    h = ffn(h, 1, 1)
    return h
```

```python
import functools

import jax
import jax.numpy as jnp
from jax import lax
from jax.experimental import pallas as pl
from jax.experimental.pallas import tpu as pltpu

D_MODEL = 2048
D_FF = 5632
LRU_HEADS = 8
LRU_BLOCK = 256
CONV_W = 4
LRU_C = 8.0
N_HEADS = 16
HEAD_DIM = 128
MOBA_BLOCK = 256
MOBA_TOPK = 3
EPS = 1e-6

F32 = jnp.float32
BF16 = jnp.bfloat16

SUBLANES = 8
LANES = 128
VMEM_LIMIT_BYTES = 56 * 1024 * 1024
MASK_VALUE = -1e30
LOG2_E = 1.4426950408889634


def _params(n_axes):
    return pltpu.CompilerParams(
        dimension_semantics=("arbitrary",) * n_axes,
        vmem_limit_bytes=VMEM_LIMIT_BYTES,
    )


def _rms_norm(x, g):
    return x * lax.rsqrt(jnp.mean(x * x, axis=-1, keepdims=True) + EPS) * g


def _dot(a, b):
    return jnp.dot(a, b, preferred_element_type=F32)


def _dot_nt(a, b):
    return lax.dot_general(a, b, (((1,), (1,)), ((), ())), preferred_element_type=F32)


FFN_TM = 1024
FFN_TF = 256


def _ffn_kernel(x_ref, g_ref, wg_ref, wu_ref, wo_ref, o_ref, xn_ref):
    f = pl.program_id(1)

    @pl.when(f == 0)
    def _():
        x = x_ref[...]
        xn_ref[...] = _rms_norm(x, g_ref[...]).astype(BF16)
        o_ref[...] = x

    xn = xn_ref[...]
    gate = _dot(xn, wg_ref[...].astype(BF16))
    up = _dot(xn, wu_ref[...].astype(BF16))
    act = (0.5 * gate) * jax.nn.sigmoid(gate) * up
    o_ref[...] += _dot(act.astype(BF16), wo_ref[...].astype(BF16))


def _ffn(x2d, g, w_in, w_out, layer, which):
    m = x2d.shape[0]
    n_f = D_FF // FFN_TF
    return pl.pallas_call(
        _ffn_kernel,
        grid=(m // FFN_TM, n_f),
        in_specs=[
            pl.BlockSpec((FFN_TM, D_MODEL), lambda i, f: (i, 0)),
            pl.BlockSpec((1, D_MODEL), lambda i, f: (0, 0)),
            pl.BlockSpec((None, None, D_MODEL, FFN_TF), lambda i, f: (layer, which, 0, f)),
            pl.BlockSpec((None, None, D_MODEL, FFN_TF), lambda i, f: (layer, which, 0, f + n_f)),
            pl.BlockSpec((None, None, FFN_TF, D_MODEL), lambda i, f: (layer, which, f, 0)),
        ],
        out_specs=pl.BlockSpec((FFN_TM, D_MODEL), lambda i, f: (i, 0)),
        out_shape=jax.ShapeDtypeStruct((m, D_MODEL), F32),
        scratch_shapes=[pltpu.VMEM((FFN_TM, D_MODEL), BF16)],
        compiler_params=_params(2),
        name="ffn",
    )(x2d, g.reshape(1, D_MODEL), w_in, w_in, w_out)


LRU_T = 512


def _softplus(x):
    return jnp.maximum(x, 0.0) + jnp.log1p(jnp.exp(-jnp.abs(x)))


def _lru_kernel(h_ref, g_ref, wx_ref, wy_ref, cw_ref, cb_ref, gw_ref, gb_ref, lam_ref, wo_ref,
                o_ref, xn_ref, carry_ref, halo_ref, hs_ref):
    s = pl.program_id(1)
    hd = pl.program_id(2)
    t, c = LRU_T, LRU_BLOCK

    @pl.when(hd == 0)
    def _():
        x = h_ref[0]
        xn_ref[...] = _rms_norm(x, g_ref[...]).astype(BF16)
        o_ref[0] = x

    @pl.when(s == 0)
    def _():
        carry_ref[hd] = jnp.zeros((SUBLANES, c), F32)
        halo_ref[hd] = jnp.zeros((SUBLANES, c), F32)

    xn = xn_ref[...]
    xb = _dot(xn, wx_ref[...])
    yb = _dot(xn, wy_ref[...])

    ext = jnp.concatenate([halo_ref[hd], xb], axis=0)
    halo_ref[hd] = xb[t - SUBLANES:]
    cw = cw_ref[...]
    xc = jnp.broadcast_to(cb_ref[...], (t, c))
    for tap in range(CONV_W):
        back = CONV_W - 1 - tap
        shifted = ext if back == 0 else pltpu.roll(ext, back, 0)
        xc = xc + shifted[SUBLANES:] * cw[tap:tap + 1]

    xcb = xc.astype(BF16)
    gb = gb_ref[...]
    r = jax.nn.sigmoid(_dot(xcb, gw_ref[0, 0]) + gb[0:1])
    i = jax.nn.sigmoid(_dot(xcb, gw_ref[1, 0]) + gb[1:2])
    log_a = (-LRU_C * r) * _softplus(-lam_ref[...])
    a = jnp.exp(log_a)
    u = xc * i * jnp.exp2(0.5 * jnp.log2(-jnp.tanh(log_a) * (a * a + 1.0)))

    row_in_group = lax.broadcasted_iota(jnp.int32, (t, c), 0) & (SUBLANES - 1)
    d = 1
    while d < SUBLANES:
        keep = row_in_group >= d
        a_prev = jnp.where(keep, pltpu.roll(a, d, 0), 1.0)
        u_prev = jnp.where(keep, pltpu.roll(u, d, 0), 0.0)
        u = a * u_prev + u
        a = a * a_prev
        d *= 2
    prev = carry_ref[hd]
    for grp in range(t // SUBLANES):
        rows = slice(grp * SUBLANES, (grp + 1) * SUBLANES)
        hg = u[rows] + a[rows] * prev
        hs_ref[rows, :] = hg
        prev = jnp.broadcast_to(hg[SUBLANES - 1:SUBLANES], (SUBLANES, c))
    carry_ref[hd] = prev

    mixed = hs_ref[...] * jax.nn.gelu(yb, approximate=True)
    o_ref[0] += _dot(mixed.astype(BF16), wo_ref[...])


def _lru(h, g, w_in, conv_w, conv_b, gate_w, gate_b, lam, w_out):
    bsz, seq, _ = h.shape
    t, c = LRU_T, LRU_BLOCK
    return pl.pallas_call(
        _lru_kernel,
        grid=(bsz, seq // t, LRU_HEADS),
        in_specs=[
            pl.BlockSpec((1, t, D_MODEL), lambda b, s, k: (b, s, 0)),
            pl.BlockSpec((1, D_MODEL), lambda b, s, k: (0, 0)),
            pl.BlockSpec((D_MODEL, c), lambda b, s, k: (0, k)),
            pl.BlockSpec((D_MODEL, c), lambda b, s, k: (0, k + LRU_HEADS)),
            pl.BlockSpec((CONV_W, c), lambda b, s, k: (0, k)),
            pl.BlockSpec((1, c), lambda b, s, k: (0, k)),
            pl.BlockSpec((2, 1, c, c), lambda b, s, k: (0, k, 0, 0)),
            pl.BlockSpec((2, c), lambda b, s, k: (0, k)),
            pl.BlockSpec((1, c), lambda b, s, k: (0, k)),
            pl.BlockSpec((c, D_MODEL), lambda b, s, k: (k, 0)),
        ],
        out_specs=pl.BlockSpec((1, t, D_MODEL), lambda b, s, k: (b, s, 0)),
        out_shape=jax.ShapeDtypeStruct(h.shape, F32),
        scratch_shapes=[
            pltpu.VMEM((t, D_MODEL), BF16),
            pltpu.VMEM((LRU_HEADS, SUBLANES, c), F32),
            pltpu.VMEM((LRU_HEADS, SUBLANES, c), F32),
            pltpu.VMEM((t, c), F32),
        ],
        compiler_params=_params(3),
        name="lru",
    )(h, g.reshape(1, D_MODEL), w_in, w_in, conv_w, conv_b.reshape(1, -1), gate_w, gate_b,
      lam.reshape(1, -1), w_out)


KV_T = 512
HEAD_GROUP = 8
GROUP_COLS = HEAD_GROUP * HEAD_DIM
AUG_DIM = 2 * HEAD_DIM


def _head_rms_norm(x, g):
    return x * lax.rsqrt(jnp.mean(x * x, axis=-1, keepdims=True) + EPS) * g


def _kv_kernel(h_ref, g_ref, wk_ref, wv_ref, kg_ref, k_ref, vt_ref, km_ref, xn_ref):
    s = pl.program_id(1)
    n = pl.program_id(2)
    t = KV_T

    @pl.when(n == 0)
    def _():
        xn_ref[...] = _rms_norm(h_ref[0], g_ref[...]).astype(BF16)

    xn = xn_ref[...]
    k = _dot(xn, wk_ref[...])
    v = _dot(xn, wv_ref[...])
    vt_ref[0] = v.T.astype(BF16)

    row = lax.broadcasted_iota(jnp.int32, (t, LANES), 0)
    lane = lax.broadcasted_iota(jnp.int32, (t, LANES), 1)
    block_of_row = s * (t // MOBA_BLOCK) + row // MOBA_BLOCK
    onehot = jnp.where(lane == block_of_row, 1.0, 0.0).astype(BF16)
    for hl in range(HEAD_GROUP):
        kh = _head_rms_norm(k[:, hl * HEAD_DIM:(hl + 1) * HEAD_DIM], kg_ref[...])
        k_ref[0, :, hl * AUG_DIM:hl * AUG_DIM + HEAD_DIM] = kh.astype(BF16)
        k_ref[0, :, hl * AUG_DIM + HEAD_DIM:(hl + 1) * AUG_DIM] = onehot
        for blk in range(t // MOBA_BLOCK):
            km_ref[0, blk, :, hl * HEAD_DIM:(hl + 1) * HEAD_DIM] = jnp.mean(
                kh[blk * MOBA_BLOCK:(blk + 1) * MOBA_BLOCK], axis=0, keepdims=True)


def _shared_kv(h, g, w_kv, k_g):
    bsz, seq, _ = h.shape
    t = KV_T
    n_groups = N_HEADS // HEAD_GROUP
    n_blocks = seq // MOBA_BLOCK
    return pl.pallas_call(
        _kv_kernel,
        grid=(bsz, seq // t, n_groups),
        in_specs=[
            pl.BlockSpec((1, t, D_MODEL), lambda b, s, n: (b, s, 0)),
            pl.BlockSpec((1, D_MODEL), lambda b, s, n: (0, 0)),
            pl.BlockSpec((D_MODEL, GROUP_COLS), lambda b, s, n: (0, n)),
            pl.BlockSpec((D_MODEL, GROUP_COLS), lambda b, s, n: (0, n + n_groups)),
            pl.BlockSpec((1, HEAD_DIM), lambda b, s, n: (0, 0)),
        ],
        out_specs=[
            pl.BlockSpec((1, t, HEAD_GROUP * AUG_DIM), lambda b, s, n: (b, s, n)),
            pl.BlockSpec((1, GROUP_COLS, t), lambda b, s, n: (b, n, s)),
            pl.BlockSpec((1, t // MOBA_BLOCK, 1, GROUP_COLS), lambda b, s, n: (b, s, 0, n)),
        ],
        out_shape=[
            jax.ShapeDtypeStruct((bsz, seq, N_HEADS * AUG_DIM), BF16),
            jax.ShapeDtypeStruct((bsz, N_HEADS * HEAD_DIM, seq), BF16),
            jax.ShapeDtypeStruct((bsz, n_blocks, 1, N_HEADS * HEAD_DIM), F32),
        ],
        scratch_shapes=[pltpu.VMEM((t, D_MODEL), BF16)],
        compiler_params=_params(3),
        name="shared_kv",
    )(h, g.reshape(1, D_MODEL), w_kv, w_kv, k_g.reshape(1, HEAD_DIM))


Q_T = 512
GATE_SLOTS = SUBLANES * 2


def _split_bf16(x):
    hi = x.astype(BF16)
    lo = (x - hi.astype(F32)).astype(BF16)
    return hi, lo


def _q_kernel(h_ref, g_ref, wqt_ref, qg_ref, km_ref, qt_ref, xn_ref):
    s = pl.program_id(1)
    n = pl.program_id(2)
    t = Q_T

    @pl.when(n == 0)
    def _():
        xn_ref[...] = _rms_norm(h_ref[0], g_ref[...]).astype(BF16)

    q_t = _dot_nt(wqt_ref[...], xn_ref[...])
    qg = jnp.broadcast_to(qg_ref[...], (HEAD_DIM, t))
    heads = []
    for hl in range(HEAD_GROUP):
        x = q_t[hl * HEAD_DIM:(hl + 1) * HEAD_DIM]
        heads.append(x * lax.rsqrt(jnp.mean(x * x, axis=0, keepdims=True) + EPS) * qg)
    qn_t = jnp.concatenate(heads, axis=0)

    km = km_ref[0]
    km_rows = jnp.concatenate([km] * HEAD_GROUP, axis=0)
    r_idx = lax.broadcasted_iota(jnp.int32, (HEAD_GROUP * GATE_SLOTS, GROUP_COLS), 0)
    c_idx = lax.broadcasted_iota(jnp.int32, (HEAD_GROUP * GATE_SLOTS, GROUP_COLS), 1)
    km_diag = jnp.where(r_idx // GATE_SLOTS == c_idx // HEAD_DIM, km_rows, 0.0)
    q_hi, q_lo = _split_bf16(qn_t)
    k_hi, k_lo = _split_bf16(km_diag)
    gate = _dot(k_hi, q_hi) + (_dot(k_lo, q_hi) + _dot(k_hi, q_lo))

    blk = lax.broadcasted_iota(jnp.int32, (GATE_SLOTS, t), 0)
    col = lax.broadcasted_iota(jnp.int32, (GATE_SLOTS, t), 1)
    own = s * (t // MOBA_BLOCK) + col // MOBA_BLOCK
    past = blk < own
    zeros_tail = jnp.zeros((AUG_DIM - HEAD_DIM - GATE_SLOTS, t), BF16)
    for hl in range(HEAD_GROUP):
        g = jnp.where(past, gate[hl * GATE_SLOTS:(hl + 1) * GATE_SLOTS], -jnp.inf)
        rank = jnp.zeros((GATE_SLOTS, t), jnp.int32)
        for other in range(GATE_SLOTS):
            go = jnp.broadcast_to(g[other:other + 1], (GATE_SLOTS, t))
            beats = (go > g) | ((go == g) & (blk > other))
            rank += jnp.where(beats, 1, 0)
        attend = ((rank < MOBA_TOPK) & past) | (blk >= own)
        bias = jnp.where(attend, 0.0, MASK_VALUE)
        base = hl * AUG_DIM
        qt_ref[0, base:base + HEAD_DIM, :] = heads[hl].astype(BF16)
        qt_ref[0, base + HEAD_DIM:base + HEAD_DIM + GATE_SLOTS, :] = bias.astype(BF16)
        qt_ref[0, base + HEAD_DIM + GATE_SLOTS:base + AUG_DIM, :] = zeros_tail


def _queries(h, g, w_q_t, q_g, k_mean):
    bsz, seq, _ = h.shape
    t = Q_T
    n_groups = N_HEADS // HEAD_GROUP
    n_blocks = seq // MOBA_BLOCK
    assert n_blocks == GATE_SLOTS
    return pl.pallas_call(
        _q_kernel,
        grid=(bsz, seq // t, n_groups),
        in_specs=[
            pl.BlockSpec((1, t, D_MODEL), lambda b, s, n: (b, s, 0)),
            pl.BlockSpec((1, D_MODEL), lambda b, s, n: (0, 0)),
            pl.BlockSpec((GROUP_COLS, D_MODEL), lambda b, s, n: (n, 0)),
            pl.BlockSpec((HEAD_DIM, 1), lambda b, s, n: (0, 0)),
            pl.BlockSpec((1, n_blocks, GROUP_COLS), lambda b, s, n: (b, 0, n)),
        ],
        out_specs=pl.BlockSpec((1, HEAD_GROUP * AUG_DIM, t), lambda b, s, n: (b, n, s)),
        out_shape=jax.ShapeDtypeStruct((bsz, N_HEADS * AUG_DIM, seq), BF16),
        scratch_shapes=[pltpu.VMEM((t, D_MODEL), BF16)],
        compiler_params=_params(3),
        name="queries",
    )(h, g.reshape(1, D_MODEL), w_q_t, q_g.reshape(HEAD_DIM, 1),
      k_mean.reshape(bsz, n_blocks, N_HEADS * HEAD_DIM))


def _attn_kernel(qt_ref, k_ref, vt_ref, o_ref, m_ref, l_ref, acc_ref):
    blk = MOBA_BLOCK
    n_blocks = qt_ref.shape[2] // blk
    scale = HEAD_DIM ** -0.5 * LOG2_E
    k_pos = lax.broadcasted_iota(jnp.int32, (blk, blk), 0)
    q_pos = lax.broadcasted_iota(jnp.int32, (blk, blk), 1)
    causal = k_pos <= q_pos

    def rows(j):
        return slice(j * blk, (j + 1) * blk)

    own = [_dot(k_ref[0, rows(i), :], qt_ref[0, :, rows(i)]) for i in range(n_blocks)]
    for i in range(n_blocks):
        s = jnp.where(causal, own[i] * scale, MASK_VALUE)
        m = jnp.max(s, axis=0, keepdims=True)
        p = jnp.exp2(s - m)
        m_ref[:, rows(i)] = m
        l_ref[:, rows(i)] = jnp.sum(p, axis=0, keepdims=True)
        acc_ref[:, rows(i)] = _dot(vt_ref[0, :, rows(i)], p.astype(BF16))

    for j in range(n_blocks - 1):
        later = slice((j + 1) * blk, n_blocks * blk)
        s = _dot(k_ref[0, rows(j), :], qt_ref[0, :, later]) * scale
        m_old = m_ref[:, later]
        m_new = jnp.maximum(m_old, jnp.max(s, axis=0, keepdims=True))
        alpha = jnp.exp2(m_old - m_new)
        p = jnp.exp2(s - m_new)
        m_ref[:, later] = m_new
        l_ref[:, later] = alpha * l_ref[:, later] + jnp.sum(p, axis=0, keepdims=True)
        acc_ref[:, later] = alpha * acc_ref[:, later] + _dot(vt_ref[0, :, rows(j)], p.astype(BF16))

    o = acc_ref[...] * (1.0 / l_ref[...])
    o_ref[0] = o.T.astype(BF16)


def _attention(q_aug, k_aug, v_t):
    bsz, _, seq = v_t.shape
    n_blocks = seq // MOBA_BLOCK
    return pl.pallas_call(
        _attn_kernel,
        grid=(bsz, N_HEADS),
        in_specs=[
            pl.BlockSpec((1, AUG_DIM, seq), lambda b, h: (b, h, 0)),
            pl.BlockSpec((1, seq, AUG_DIM), lambda b, h: (b, 0, h)),
            pl.BlockSpec((1, HEAD_DIM, seq), lambda b, h: (b, h, 0)),
        ],
        out_specs=pl.BlockSpec((1, seq, HEAD_DIM), lambda b, h: (b, 0, h)),
        out_shape=jax.ShapeDtypeStruct((bsz, seq, N_HEADS * HEAD_DIM), BF16),
        scratch_shapes=[
            pltpu.VMEM((1, seq), F32),
            pltpu.VMEM((1, seq), F32),
            pltpu.VMEM((HEAD_DIM, seq), F32),
        ],
        compiler_params=_params(2),
        name="moba_attention",
    )(q_aug, k_aug, v_t)


PROJ_TM = 1024
PROJ_TN = 512


def _out_proj_kernel(o_ref, w_ref, x_ref, y_ref):
    y_ref[...] = x_ref[...] + _dot(o_ref[...], w_ref[...])


def _out_proj(o2d, w_o, x2d):
    m, k = o2d.shape
    n = w_o.shape[1]
    return pl.pallas_call(
        _out_proj_kernel,
        grid=(m // PROJ_TM, n // PROJ_TN),
        in_specs=[
            pl.BlockSpec((PROJ_TM, k), lambda i, j: (i, 0)),
            pl.BlockSpec((k, PROJ_TN), lambda i, j: (0, j)),
            pl.BlockSpec((PROJ_TM, PROJ_TN), lambda i, j: (i, j)),
        ],
        out_specs=pl.BlockSpec((PROJ_TM, PROJ_TN), lambda i, j: (i, j)),
        out_shape=jax.ShapeDtypeStruct((m, n), F32),
        compiler_params=_params(2),
        name="out_proj",
    )(o2d, w_o, x2d)


def kernel(x, norm_g, ffn_w_in, ffn_w_out, lru_w_in, lru_conv_w, lru_conv_b, lru_gate_w, lru_gate_b, lru_lambda, lru_w_out, kv_norm_g, w_kv, k_norm_g, attn_w_q, q_norm_g, attn_w_o):
    bsz, seq, d = x.shape
    rows = bsz * seq

    def ffn(h, layer, which):
        return _ffn(h.reshape(rows, d), norm_g[layer, 2 * which], ffn_w_in, ffn_w_out,
                    layer, which).reshape(bsz, seq, d)

    h = ffn(x, 0, 0)
    h = _lru(h, norm_g[0, 1], lru_w_in[0].astype(BF16), lru_conv_w[0], lru_conv_b[0],
             lru_gate_w[0].astype(BF16), lru_gate_b[0], lru_lambda[0], lru_w_out[0].astype(BF16))
    h = ffn(h, 0, 1)
    k_aug, v_t, k_mean = _shared_kv(h, kv_norm_g, w_kv.astype(BF16), k_norm_g)

    h = ffn(h, 1, 0)
    q_aug_t = _queries(h, norm_g[1, 1], attn_w_q[0].T.astype(BF16), q_norm_g[0], k_mean)
    o = _attention(q_aug_t, k_aug, v_t)
    h = _out_proj(o.reshape(rows, d), attn_w_o[0].astype(BF16), h.reshape(rows, d)).reshape(bsz, seq, d)
    h = ffn(h, 1, 1)
    return h
```

```python
import functools

import jax
import jax.numpy as jnp
from jax import lax
from jax.experimental import pallas as pl
from jax.experimental.pallas import tpu as pltpu

D_MODEL = 2048
D_FF = 5632
LRU_HEADS = 8
LRU_BLOCK = 256
CONV_W = 4
LRU_C = 8.0
N_HEADS = 16
HEAD_DIM = 128
MOBA_BLOCK = 256
MOBA_TOPK = 3
EPS = 1e-6

F32 = jnp.float32
BF16 = jnp.bfloat16

SUBLANES = 8
LANES = 128
VMEM_LIMIT_BYTES = 56 * 1024 * 1024
MASK_VALUE = -1e30
LOG2_E = 1.4426950408889634
QK_SCALE = HEAD_DIM ** -0.5 * LOG2_E


def _params(n_axes):
    return pltpu.CompilerParams(
        dimension_semantics=("arbitrary",) * n_axes,
        vmem_limit_bytes=VMEM_LIMIT_BYTES,
    )


def _rms_norm(x, g):
    return x * lax.rsqrt(jnp.mean(x * x, axis=-1, keepdims=True) + EPS) * g


def _dot(a, b):
    return jnp.dot(a, b, preferred_element_type=F32)


def _dot_nt(a, b):
    return lax.dot_general(a, b, (((1,), (1,)), ((), ())), preferred_element_type=F32)


FFN_TM = 1024
FFN_TF = 256


def _ffn_kernel(x_ref, g_ref, wg_ref, wu_ref, wo_ref, o_ref, xn_ref):
    f = pl.program_id(1)

    @pl.when(f == 0)
    def _():
        x = x_ref[...]
        xn_ref[...] = _rms_norm(x, g_ref[...]).astype(BF16)
        o_ref[...] = x

    xn = xn_ref[...]
    gate = _dot(xn, wg_ref[...].astype(BF16))
    up = _dot(xn, wu_ref[...].astype(BF16))
    act = (0.5 * gate) * jax.nn.sigmoid(gate) * up
    o_ref[...] += _dot(act.astype(BF16), wo_ref[...].astype(BF16))


def _ffn(x2d, g, w_in, w_out, layer, which):
    m = x2d.shape[0]
    n_f = D_FF // FFN_TF
    return pl.pallas_call(
        _ffn_kernel,
        grid=(m // FFN_TM, n_f),
        in_specs=[
            pl.BlockSpec((FFN_TM, D_MODEL), lambda i, f: (i, 0)),
            pl.BlockSpec((1, D_MODEL), lambda i, f: (0, 0)),
            pl.BlockSpec((None, None, D_MODEL, FFN_TF), lambda i, f: (layer, which, 0, f)),
            pl.BlockSpec((None, None, D_MODEL, FFN_TF), lambda i, f: (layer, which, 0, f + n_f)),
            pl.BlockSpec((None, None, FFN_TF, D_MODEL), lambda i, f: (layer, which, f, 0)),
        ],
        out_specs=pl.BlockSpec((FFN_TM, D_MODEL), lambda i, f: (i, 0)),
        out_shape=jax.ShapeDtypeStruct((m, D_MODEL), F32),
        scratch_shapes=[pltpu.VMEM((FFN_TM, D_MODEL), BF16)],
        compiler_params=_params(2),
        name="ffn",
    )(x2d, g.reshape(1, D_MODEL), w_in, w_in, w_out)


LRU_T = 512


def _softplus(x):
    return jnp.maximum(x, 0.0) + jnp.log1p(jnp.exp(-jnp.abs(x)))


def _lru_kernel(h_ref, g_ref, wx_ref, wy_ref, cw_ref, cb_ref, gw_ref, gb_ref, lam_ref, wo_ref,
                o_ref, xn_ref, xc_ref, yb_ref, rp_ref, ip_ref, carry_ref, halo_ref, hs_ref):
    s = pl.program_id(1)
    k = pl.program_id(2)
    t, c = LRU_T, LRU_BLOCK

    def front(hd, slot):
        xn = xn_ref[...]
        xb = _dot(xn, wx_ref[...])
        yb = _dot(xn, wy_ref[...])
        ext = jnp.concatenate([halo_ref[hd], xb], axis=0)
        cw = cw_ref[...]
        xc = jnp.broadcast_to(cb_ref[...], (t, c))
        for tap in range(CONV_W):
            back = CONV_W - 1 - tap
            shifted = ext if back == 0 else pltpu.roll(ext, back, 0)
            xc = xc + shifted[SUBLANES:] * cw[tap:tap + 1]
        xcb = xc.astype(BF16)
        r_pre = _dot(xcb, gw_ref[0, 0])
        i_pre = _dot(xcb, gw_ref[1, 0])

        def store():
            halo_ref[hd] = xb[t - SUBLANES:]
            yb_ref[slot] = yb
            xc_ref[slot] = xc
            rp_ref[slot] = r_pre
            ip_ref[slot] = i_pre
        return store

    def back(hd, slot):
        gb = gb_ref[...]
        r = jax.nn.sigmoid(rp_ref[slot] + gb[0:1])
        i = jax.nn.sigmoid(ip_ref[slot] + gb[1:2])
        log_a = (-LRU_C * r) * _softplus(-lam_ref[...])
        a = jnp.exp(log_a)
        u = xc_ref[slot] * i * jnp.exp2(0.5 * jnp.log2(-jnp.tanh(log_a) * (a * a + 1.0)))

        row_in_group = lax.broadcasted_iota(jnp.int32, (t, c), 0) & (SUBLANES - 1)
        d = 1
        while d < SUBLANES:
            keep = row_in_group >= d
            a_prev = jnp.where(keep, pltpu.roll(a, d, 0), 1.0)
            u_prev = jnp.where(keep, pltpu.roll(u, d, 0), 0.0)
            u = a * u_prev + u
            a = a * a_prev
            d *= 2
        prev = carry_ref[hd]
        for grp in range(t // SUBLANES):
            rows = slice(grp * SUBLANES, (grp + 1) * SUBLANES)
            hg = u[rows] + a[rows] * prev
            hs_ref[rows, :] = hg
            prev = jnp.broadcast_to(hg[SUBLANES - 1:SUBLANES], (SUBLANES, c))
        carry_ref[hd] = prev

        mixed = hs_ref[...] * jax.nn.gelu(yb_ref[slot], approximate=True)
        o_ref[0] += _dot(mixed.astype(BF16), wo_ref[...])

    @pl.when((s == 0) & (k == 0))
    def _():
        carry_ref[...] = jnp.zeros(carry_ref.shape, F32)
        halo_ref[...] = jnp.zeros(halo_ref.shape, F32)

    @pl.when(k == 0)
    def _():
        x = h_ref[0]
        xn_ref[...] = _rms_norm(x, g_ref[...]).astype(BF16)
        o_ref[0] = x
        front(0, 0)()

    for parity in (0, 1):
        @pl.when((k > 0) & (k < LRU_HEADS) & (k % 2 == parity))
        def _(parity=parity):
            back(k - 1, 1 - parity)
            front(k, parity)()

    @pl.when(k == LRU_HEADS)
    def _():
        back(LRU_HEADS - 1, (LRU_HEADS - 1) % 2)


def _lru(h, g, w_in, conv_w, conv_b, gate_w, gate_b, lam, w_out):
    bsz, seq, _ = h.shape
    t, c = LRU_T, LRU_BLOCK
    last = LRU_HEADS - 1

    def proj_head(k):
        return jnp.minimum(k, last)

    def mix_head(k):
        return jnp.maximum(k - 1, 0)

    return pl.pallas_call(
        _lru_kernel,
        grid=(bsz, seq // t, LRU_HEADS + 1),
        in_specs=[
            pl.BlockSpec((1, t, D_MODEL), lambda b, s, k: (b, s, 0)),
            pl.BlockSpec((1, D_MODEL), lambda b, s, k: (0, 0)),
            pl.BlockSpec((D_MODEL, c), lambda b, s, k: (0, proj_head(k))),
            pl.BlockSpec((D_MODEL, c), lambda b, s, k: (0, proj_head(k) + LRU_HEADS)),
            pl.BlockSpec((CONV_W, c), lambda b, s, k: (0, proj_head(k))),
            pl.BlockSpec((1, c), lambda b, s, k: (0, proj_head(k))),
            pl.BlockSpec((2, 1, c, c), lambda b, s, k: (0, proj_head(k), 0, 0)),
            pl.BlockSpec((2, c), lambda b, s, k: (0, mix_head(k))),
            pl.BlockSpec((1, c), lambda b, s, k: (0, mix_head(k))),
            pl.BlockSpec((c, D_MODEL), lambda b, s, k: (mix_head(k), 0)),
        ],
        out_specs=pl.BlockSpec((1, t, D_MODEL), lambda b, s, k: (b, s, 0)),
        out_shape=jax.ShapeDtypeStruct(h.shape, F32),
        scratch_shapes=[
            pltpu.VMEM((t, D_MODEL), BF16),
            pltpu.VMEM((2, t, c), F32),
            pltpu.VMEM((2, t, c), F32),
            pltpu.VMEM((2, t, c), F32),
            pltpu.VMEM((2, t, c), F32),
            pltpu.VMEM((LRU_HEADS, SUBLANES, c), F32),
            pltpu.VMEM((LRU_HEADS, SUBLANES, c), F32),
            pltpu.VMEM((t, c), F32),
        ],
        compiler_params=_params(3),
        name="lru",
    )(h, g.reshape(1, D_MODEL), w_in, w_in, conv_w, conv_b.reshape(1, -1), gate_w, gate_b,
      lam.reshape(1, -1), w_out)


KV_T = 512
HEAD_GROUP = 8
GROUP_COLS = HEAD_GROUP * HEAD_DIM
AUG_DIM = 2 * HEAD_DIM
V_AUG_DIM = HEAD_DIM + 2 * SUBLANES


def _head_rms_norm(x, g):
    return x * lax.rsqrt(jnp.mean(x * x, axis=-1, keepdims=True) + EPS) * g


def _kv_kernel(h_ref, g_ref, wk_ref, wv_ref, kg_ref, k_ref, vt_ref, km_ref, xn_ref):
    s = pl.program_id(1)
    n = pl.program_id(2)
    t = KV_T

    @pl.when(n == 0)
    def _():
        xn_ref[...] = _rms_norm(h_ref[0], g_ref[...]).astype(BF16)

    xn = xn_ref[...]
    k = _dot(xn, wk_ref[...])
    v = _dot(xn, wv_ref[...])
    v_t = v.T.astype(BF16)
    ones_rows = jnp.where(lax.broadcasted_iota(jnp.int32, (V_AUG_DIM - HEAD_DIM, t), 0) == 0,
                          1.0, 0.0).astype(BF16)
    for hl in range(HEAD_GROUP):
        vt_ref[0, hl * V_AUG_DIM:hl * V_AUG_DIM + HEAD_DIM, :] = v_t[hl * HEAD_DIM:(hl + 1) * HEAD_DIM]
        vt_ref[0, hl * V_AUG_DIM + HEAD_DIM:(hl + 1) * V_AUG_DIM, :] = ones_rows

    row = lax.broadcasted_iota(jnp.int32, (t, LANES), 0)
    lane = lax.broadcasted_iota(jnp.int32, (t, LANES), 1)
    block_of_row = s * (t // MOBA_BLOCK) + row // MOBA_BLOCK
    onehot = jnp.where(lane == block_of_row, 1.0, 0.0).astype(BF16)
    for hl in range(HEAD_GROUP):
        kh = _head_rms_norm(k[:, hl * HEAD_DIM:(hl + 1) * HEAD_DIM], kg_ref[...])
        k_ref[0, :, hl * AUG_DIM:hl * AUG_DIM + HEAD_DIM] = kh.astype(BF16)
        k_ref[0, :, hl * AUG_DIM + HEAD_DIM:(hl + 1) * AUG_DIM] = onehot
        for blk in range(t // MOBA_BLOCK):
            km_ref[0, blk, :, hl * HEAD_DIM:(hl + 1) * HEAD_DIM] = jnp.mean(
                kh[blk * MOBA_BLOCK:(blk + 1) * MOBA_BLOCK], axis=0, keepdims=True)


def _shared_kv(h, g, w_kv, k_g):
    bsz, seq, _ = h.shape
    t = KV_T
    n_groups = N_HEADS // HEAD_GROUP
    n_blocks = seq // MOBA_BLOCK
    return pl.pallas_call(
        _kv_kernel,
        grid=(bsz, seq // t, n_groups),
        in_specs=[
            pl.BlockSpec((1, t, D_MODEL), lambda b, s, n: (b, s, 0)),
            pl.BlockSpec((1, D_MODEL), lambda b, s, n: (0, 0)),
            pl.BlockSpec((D_MODEL, GROUP_COLS), lambda b, s, n: (0, n)),
            pl.BlockSpec((D_MODEL, GROUP_COLS), lambda b, s, n: (0, n + n_groups)),
            pl.BlockSpec((1, HEAD_DIM), lambda b, s, n: (0, 0)),
        ],
        out_specs=[
            pl.BlockSpec((1, t, HEAD_GROUP * AUG_DIM), lambda b, s, n: (b, s, n)),
            pl.BlockSpec((1, HEAD_GROUP * V_AUG_DIM, t), lambda b, s, n: (b, n, s)),
            pl.BlockSpec((1, t // MOBA_BLOCK, 1, GROUP_COLS), lambda b, s, n: (b, s, 0, n)),
        ],
        out_shape=[
            jax.ShapeDtypeStruct((bsz, seq, N_HEADS * AUG_DIM), BF16),
            jax.ShapeDtypeStruct((bsz, N_HEADS * V_AUG_DIM, seq), BF16),
            jax.ShapeDtypeStruct((bsz, n_blocks, 1, N_HEADS * HEAD_DIM), F32),
        ],
        scratch_shapes=[pltpu.VMEM((t, D_MODEL), BF16)],
        compiler_params=_params(3),
        name="shared_kv",
    )(h, g.reshape(1, D_MODEL), w_kv, w_kv, k_g.reshape(1, HEAD_DIM))


Q_T = 512
GATE_SLOTS = SUBLANES * 2


def _split_bf16(x):
    hi = x.astype(BF16)
    lo = (x - hi.astype(F32)).astype(BF16)
    return hi, lo


def _q_kernel(h_ref, g_ref, wqt_ref, qg_ref, km_ref, qt_ref, xn_ref):
    s = pl.program_id(1)
    n = pl.program_id(2)
    t = Q_T

    @pl.when(n == 0)
    def _():
        xn_ref[...] = _rms_norm(h_ref[0], g_ref[...]).astype(BF16)

    q_t = _dot_nt(wqt_ref[...], xn_ref[...])
    qg = jnp.broadcast_to(qg_ref[...], (HEAD_DIM, t))
    heads = []
    for hl in range(HEAD_GROUP):
        x = q_t[hl * HEAD_DIM:(hl + 1) * HEAD_DIM]
        heads.append(x * lax.rsqrt(jnp.mean(x * x, axis=0, keepdims=True) + EPS) * qg)
    qn_t = jnp.concatenate(heads, axis=0)

    km = km_ref[0]
    km_rows = jnp.concatenate([km] * HEAD_GROUP, axis=0)
    r_idx = lax.broadcasted_iota(jnp.int32, (HEAD_GROUP * GATE_SLOTS, GROUP_COLS), 0)
    c_idx = lax.broadcasted_iota(jnp.int32, (HEAD_GROUP * GATE_SLOTS, GROUP_COLS), 1)
    km_diag = jnp.where(r_idx // GATE_SLOTS == c_idx // HEAD_DIM, km_rows, 0.0)
    q_hi, q_lo = _split_bf16(qn_t)
    k_hi, k_lo = _split_bf16(km_diag)
    gate = _dot(k_hi, q_hi) + (_dot(k_lo, q_hi) + _dot(k_hi, q_lo))

    blk = lax.broadcasted_iota(jnp.int32, (GATE_SLOTS, t), 0)
    col = lax.broadcasted_iota(jnp.int32, (GATE_SLOTS, t), 1)
    own = s * (t // MOBA_BLOCK) + col // MOBA_BLOCK
    past = blk < own
    zeros_tail = jnp.zeros((AUG_DIM - HEAD_DIM - GATE_SLOTS, t), BF16)
    for hl in range(HEAD_GROUP):
        g = jnp.where(past, gate[hl * GATE_SLOTS:(hl + 1) * GATE_SLOTS], -jnp.inf)
        rank = jnp.zeros((GATE_SLOTS, t), jnp.int32)
        for other in range(GATE_SLOTS):
            go = jnp.broadcast_to(g[other:other + 1], (GATE_SLOTS, t))
            beats = (go > g) | ((go == g) & (blk > other))
            rank += jnp.where(beats, 1, 0)
        attend = ((rank < MOBA_TOPK) & past) | (blk >= own)
        bias = jnp.where(attend, 0.0, MASK_VALUE)
        base = hl * AUG_DIM
        qt_ref[0, base:base + HEAD_DIM, :] = (heads[hl] * QK_SCALE).astype(BF16)
        qt_ref[0, base + HEAD_DIM:base + HEAD_DIM + GATE_SLOTS, :] = bias.astype(BF16)
        qt_ref[0, base + HEAD_DIM + GATE_SLOTS:base + AUG_DIM, :] = zeros_tail


def _queries(h, g, w_q_t, q_g, k_mean):
    bsz, seq, _ = h.shape
    t = Q_T
    n_groups = N_HEADS // HEAD_GROUP
    n_blocks = seq // MOBA_BLOCK
    assert n_blocks == GATE_SLOTS
    return pl.pallas_call(
        _q_kernel,
        grid=(bsz, seq // t, n_groups),
        in_specs=[
            pl.BlockSpec((1, t, D_MODEL), lambda b, s, n: (b, s, 0)),
            pl.BlockSpec((1, D_MODEL), lambda b, s, n: (0, 0)),
            pl.BlockSpec((GROUP_COLS, D_MODEL), lambda b, s, n: (n, 0)),
            pl.BlockSpec((HEAD_DIM, 1), lambda b, s, n: (0, 0)),
            pl.BlockSpec((1, n_blocks, GROUP_COLS), lambda b, s, n: (b, 0, n)),
        ],
        out_specs=pl.BlockSpec((1, HEAD_GROUP * AUG_DIM, t), lambda b, s, n: (b, n, s)),
        out_shape=jax.ShapeDtypeStruct((bsz, N_HEADS * AUG_DIM, seq), BF16),
        scratch_shapes=[pltpu.VMEM((t, D_MODEL), BF16)],
        compiler_params=_params(3),
        name="queries",
    )(h, g.reshape(1, D_MODEL), w_q_t, q_g.reshape(HEAD_DIM, 1),
      k_mean.reshape(bsz, n_blocks, N_HEADS * HEAD_DIM))


def _attn_kernel(qt_ref, k_ref, vt_ref, o_ref, m_ref, acc_ref):
    blk = MOBA_BLOCK
    n_blocks = qt_ref.shape[2] // blk
    k_pos = lax.broadcasted_iota(jnp.int32, (blk, blk), 0)
    q_pos = lax.broadcasted_iota(jnp.int32, (blk, blk), 1)
    causal = k_pos <= q_pos

    def rows(j):
        return slice(j * blk, (j + 1) * blk)

    own = [_dot(k_ref[0, rows(i), :], qt_ref[0, :, rows(i)]) for i in range(n_blocks)]
    for i in range(n_blocks):
        s = jnp.where(causal, own[i], MASK_VALUE)
        m = jnp.max(s, axis=0, keepdims=True)
        p = jnp.exp2(s - m)
        m_ref[:, rows(i)] = m
        acc_ref[:, rows(i)] = _dot(vt_ref[0, :, rows(i)], p.astype(BF16))

    for j in range(n_blocks - 1):
        later = slice((j + 1) * blk, n_blocks * blk)
        s = _dot(k_ref[0, rows(j), :], qt_ref[0, :, later])
        m_old = m_ref[:, later]
        m_new = jnp.maximum(m_old, jnp.max(s, axis=0, keepdims=True))
        alpha = jnp.exp2(m_old - m_new)
        p = jnp.exp2(s - m_new)
        m_ref[:, later] = m_new
        acc_ref[:, later] = alpha * acc_ref[:, later] + _dot(vt_ref[0, :, rows(j)], p.astype(BF16))

    o = acc_ref[:HEAD_DIM, :] * (1.0 / acc_ref[HEAD_DIM:HEAD_DIM + 1, :])
    o_ref[0] = o.T.astype(BF16)


def _attention(q_aug_t, k_aug, v_aug_t):
    bsz, _, seq = v_aug_t.shape
    return pl.pallas_call(
        _attn_kernel,
        grid=(bsz, N_HEADS),
        in_specs=[
            pl.BlockSpec((1, AUG_DIM, seq), lambda b, h: (b, h, 0)),
            pl.BlockSpec((1, seq, AUG_DIM), lambda b, h: (b, 0, h)),
            pl.BlockSpec((1, V_AUG_DIM, seq), lambda b, h: (b, h, 0)),
        ],
        out_specs=pl.BlockSpec((1, seq, HEAD_DIM), lambda b, h: (b, 0, h)),
        out_shape=jax.ShapeDtypeStruct((bsz, seq, N_HEADS * HEAD_DIM), BF16),
        scratch_shapes=[
            pltpu.VMEM((1, seq), F32),
            pltpu.VMEM((V_AUG_DIM, seq), F32),
        ],
        compiler_params=_params(2),
        name="moba_attention",
    )(q_aug_t, k_aug, v_aug_t)


PROJ_TM = 1024
PROJ_TN = 512


def _out_proj_kernel(o_ref, w_ref, x_ref, y_ref):
    y_ref[...] = x_ref[...] + _dot(o_ref[...], w_ref[...])


def _out_proj(o2d, w_o, x2d):
    m, k = o2d.shape
    n = w_o.shape[1]
    return pl.pallas_call(
        _out_proj_kernel,
        grid=(m // PROJ_TM, n // PROJ_TN),
        in_specs=[
            pl.BlockSpec((PROJ_TM, k), lambda i, j: (i, 0)),
            pl.BlockSpec((k, PROJ_TN), lambda i, j: (0, j)),
            pl.BlockSpec((PROJ_TM, PROJ_TN), lambda i, j: (i, j)),
        ],
        out_specs=pl.BlockSpec((PROJ_TM, PROJ_TN), lambda i, j: (i, j)),
        out_shape=jax.ShapeDtypeStruct((m, n), F32),
        compiler_params=_params(2),
        name="out_proj",
    )(o2d, w_o, x2d)


def kernel(x, norm_g, ffn_w_in, ffn_w_out, lru_w_in, lru_conv_w, lru_conv_b, lru_gate_w, lru_gate_b, lru_lambda, lru_w_out, kv_norm_g, w_kv, k_norm_g, attn_w_q, q_norm_g, attn_w_o):
    bsz, seq, d = x.shape
    rows = bsz * seq

    def ffn(h, layer, which):
        return _ffn(h.reshape(rows, d), norm_g[layer, 2 * which], ffn_w_in, ffn_w_out,
                    layer, which).reshape(bsz, seq, d)

    h = ffn(x, 0, 0)
    h = _lru(h, norm_g[0, 1], lru_w_in[0].astype(BF16), lru_conv_w[0], lru_conv_b[0],
             lru_gate_w[0].astype(BF16), lru_gate_b[0], lru_lambda[0], lru_w_out[0].astype(BF16))
    h = ffn(h, 0, 1)
    k_aug, v_t, k_mean = _shared_kv(h, kv_norm_g, w_kv.astype(BF16), k_norm_g)

    h = ffn(h, 1, 0)
    q_aug_t = _queries(h, norm_g[1, 1], attn_w_q[0].T.astype(BF16), q_norm_g[0], k_mean)
    o = _attention(q_aug_t, k_aug, v_t)
    h = _out_proj(o.reshape(rows, d), attn_w_o[0].astype(BF16), h.reshape(rows, d)).reshape(bsz, seq, d)
    h = ffn(h, 1, 1)
    return h
```

```python
import functools

import jax
import jax.numpy as jnp
from jax import lax
from jax.experimental import pallas as pl
from jax.experimental.pallas import tpu as pltpu

D_MODEL = 2048
D_FF = 5632
LRU_HEADS = 8
LRU_BLOCK = 256
LRU_WIDTH = LRU_HEADS * LRU_BLOCK
CONV_W = 4
LRU_C = 8.0
N_HEADS = 16
HEAD_DIM = 128
MOBA_BLOCK = 256
MOBA_TOPK = 3
EPS = 1e-6

F32 = jnp.float32
BF16 = jnp.bfloat16

SUBLANES = 8
LANES = 128
VMEM_LIMIT_BYTES = 56 * 1024 * 1024
MASK_VALUE = -1e30
LOG2_E = 1.4426950408889634
QK_SCALE = HEAD_DIM ** -0.5 * LOG2_E


def _params(n_axes):
    return pltpu.CompilerParams(
        dimension_semantics=("arbitrary",) * n_axes,
        vmem_limit_bytes=VMEM_LIMIT_BYTES,
    )


def _rms_norm(x, g):
    return x * lax.rsqrt(jnp.mean(x * x, axis=-1, keepdims=True) + EPS) * g


def _dot(a, b):
    return jnp.dot(a, b, preferred_element_type=F32)


def _dot_nt(a, b):
    return lax.dot_general(a, b, (((1,), (1,)), ((), ())), preferred_element_type=F32)


FFN_TM = 1024
FFN_TF = 256


def _ffn_kernel(x_ref, g_ref, wg_ref, wu_ref, wo_ref, o_ref, xn_ref):
    f = pl.program_id(1)

    @pl.when(f == 0)
    def _():
        x = x_ref[...]
        xn_ref[...] = _rms_norm(x, g_ref[...]).astype(BF16)
        o_ref[...] = x

    xn = xn_ref[...]
    gate = _dot(xn, wg_ref[...].astype(BF16))
    up = _dot(xn, wu_ref[...].astype(BF16))
    act = (0.5 * gate) * jax.nn.sigmoid(gate) * up
    o_ref[...] += _dot(act.astype(BF16), wo_ref[...].astype(BF16))


def _ffn(x2d, g, w_in, w_out, layer, which):
    m = x2d.shape[0]
    n_f = D_FF // FFN_TF
    return pl.pallas_call(
        _ffn_kernel,
        grid=(m // FFN_TM, n_f),
        in_specs=[
            pl.BlockSpec((FFN_TM, D_MODEL), lambda i, f: (i, 0)),
            pl.BlockSpec((1, D_MODEL), lambda i, f: (0, 0)),
            pl.BlockSpec((None, None, D_MODEL, FFN_TF), lambda i, f: (layer, which, 0, f)),
            pl.BlockSpec((None, None, D_MODEL, FFN_TF), lambda i, f: (layer, which, 0, f + n_f)),
            pl.BlockSpec((None, None, FFN_TF, D_MODEL), lambda i, f: (layer, which, f, 0)),
        ],
        out_specs=pl.BlockSpec((FFN_TM, D_MODEL), lambda i, f: (i, 0)),
        out_shape=jax.ShapeDtypeStruct((m, D_MODEL), F32),
        scratch_shapes=[pltpu.VMEM((FFN_TM, D_MODEL), BF16)],
        compiler_params=_params(2),
        name="ffn",
    )(x2d, g.reshape(1, D_MODEL), w_in, w_in, w_out)


LRU_T = 512


def _softplus(x):
    return jnp.maximum(x, 0.0) + jnp.log1p(jnp.exp(-jnp.abs(x)))


def _lru_kernel(h_ref, g_ref, win_ref, cw_ref, cb_ref, gw_ref, gb_ref, lam_ref, wo_ref,
                o_ref, carry_ref, halo_ref, hs_ref):
    s = pl.program_id(1)
    t, c = LRU_T, LRU_BLOCK

    @pl.when(s == 0)
    def _():
        carry_ref[...] = jnp.zeros(carry_ref.shape, F32)
        halo_ref[...] = jnp.zeros(halo_ref.shape, F32)

    x = h_ref[0]
    xn = _rms_norm(x, g_ref[...]).astype(BF16)
    o_ref[0] = x

    def cols(hd):
        return slice(hd * c, (hd + 1) * c)

    def front(hd):
        xb = _dot(xn, win_ref[:, cols(hd)])
        yb = _dot(xn, win_ref[:, cols(hd + LRU_HEADS)])
        ext = jnp.concatenate([halo_ref[hd], xb], axis=0)
        halo_ref[hd] = xb[t - SUBLANES:]
        cw = cw_ref[:, cols(hd)]
        xc = jnp.broadcast_to(cb_ref[:, cols(hd)], (t, c))
        for tap in range(CONV_W):
            back = CONV_W - 1 - tap
            shifted = ext if back == 0 else pltpu.roll(ext, back, 0)
            xc = xc + shifted[SUBLANES:] * cw[tap:tap + 1]
        xcb = xc.astype(BF16)
        return xc, yb, _dot(xcb, gw_ref[0, hd]), _dot(xcb, gw_ref[1, hd])

    def back(hd, xc, yb, r_pre, i_pre):
        gb = gb_ref[:, cols(hd)]
        r = jax.nn.sigmoid(r_pre + gb[0:1])
        i = jax.nn.sigmoid(i_pre + gb[1:2])
        log_a = (-LRU_C * r) * _softplus(-lam_ref[:, cols(hd)])
        a = jnp.exp(log_a)
        u = xc * i * jnp.exp2(0.5 * jnp.log2(-jnp.tanh(log_a) * (a * a + 1.0)))

        row_in_group = lax.broadcasted_iota(jnp.int32, (t, c), 0) & (SUBLANES - 1)
        d = 1
        while d < SUBLANES:
            keep = row_in_group >= d
            a_prev = jnp.where(keep, pltpu.roll(a, d, 0), 1.0)
            u_prev = jnp.where(keep, pltpu.roll(u, d, 0), 0.0)
            u = a * u_prev + u
            a = a * a_prev
            d *= 2
        prev = carry_ref[hd]
        for grp in range(t // SUBLANES):
            rows = slice(grp * SUBLANES, (grp + 1) * SUBLANES)
            hg = u[rows] + a[rows] * prev
            hs_ref[rows, :] = hg
            prev = jnp.broadcast_to(hg[SUBLANES - 1:SUBLANES], (SUBLANES, c))
        carry_ref[hd] = prev

        mixed = hs_ref[...] * jax.nn.gelu(yb, approximate=True)
        o_ref[0] += _dot(mixed.astype(BF16), wo_ref[cols(hd), :])

    pending = front(0)
    for hd in range(LRU_HEADS):
        current = pending
        if hd + 1 < LRU_HEADS:
            pending = front(hd + 1)
        back(hd, *current)


def _lru(h, g, w_in, conv_w, conv_b, gate_w, gate_b, lam, w_out):
    bsz, seq, _ = h.shape
    t, c = LRU_T, LRU_BLOCK

    def resident(shape):
        return pl.BlockSpec(shape, lambda b, s: (0,) * len(shape), pipeline_mode=pl.Buffered(1))

    return pl.pallas_call(
        _lru_kernel,
        grid=(bsz, seq // t),
        in_specs=[
            pl.BlockSpec((1, t, D_MODEL), lambda b, s: (b, s, 0)),
            resident((1, D_MODEL)),
            resident(w_in.shape),
            resident(conv_w.shape),
            resident((1, LRU_WIDTH)),
            resident(gate_w.shape),
            resident(gate_b.shape),
            resident((1, LRU_WIDTH)),
            resident(w_out.shape),
        ],
        out_specs=pl.BlockSpec((1, t, D_MODEL), lambda b, s: (b, s, 0)),
        out_shape=jax.ShapeDtypeStruct(h.shape, F32),
        scratch_shapes=[
            pltpu.VMEM((LRU_HEADS, SUBLANES, c), F32),
            pltpu.VMEM((LRU_HEADS, SUBLANES, c), F32),
            pltpu.VMEM((t, c), F32),
        ],
        compiler_params=_params(2),
        name="lru",
    )(h, g.reshape(1, D_MODEL), w_in, conv_w, conv_b.reshape(1, -1), gate_w, gate_b,
      lam.reshape(1, -1), w_out)


KV_T = 512
HEAD_GROUP = 8
GROUP_COLS = HEAD_GROUP * HEAD_DIM
AUG_DIM = 2 * HEAD_DIM
V_AUG_DIM = HEAD_DIM + 2 * SUBLANES


def _head_rms_norm(x, g):
    return x * lax.rsqrt(jnp.mean(x * x, axis=-1, keepdims=True) + EPS) * g


def _kv_kernel(h_ref, g_ref, wk_ref, wv_ref, kg_ref, k_ref, vt_ref, km_ref, xn_ref):
    s = pl.program_id(1)
    n = pl.program_id(2)
    t = KV_T

    @pl.when(n == 0)
    def _():
        xn_ref[...] = _rms_norm(h_ref[0], g_ref[...]).astype(BF16)

    xn = xn_ref[...]
    k = _dot(xn, wk_ref[...])
    v = _dot(xn, wv_ref[...])
    v_t = v.T.astype(BF16)
    ones_rows = jnp.where(lax.broadcasted_iota(jnp.int32, (V_AUG_DIM - HEAD_DIM, t), 0) == 0,
                          1.0, 0.0).astype(BF16)
    for hl in range(HEAD_GROUP):
        vt_ref[0, hl * V_AUG_DIM:hl * V_AUG_DIM + HEAD_DIM, :] = v_t[hl * HEAD_DIM:(hl + 1) * HEAD_DIM]
        vt_ref[0, hl * V_AUG_DIM + HEAD_DIM:(hl + 1) * V_AUG_DIM, :] = ones_rows

    row = lax.broadcasted_iota(jnp.int32, (t, LANES), 0)
    lane = lax.broadcasted_iota(jnp.int32, (t, LANES), 1)
    block_of_row = s * (t // MOBA_BLOCK) + row // MOBA_BLOCK
    onehot = jnp.where(lane == block_of_row, 1.0, 0.0).astype(BF16)
    for hl in range(HEAD_GROUP):
        kh = _head_rms_norm(k[:, hl * HEAD_DIM:(hl + 1) * HEAD_DIM], kg_ref[...])
        k_ref[0, :, hl * AUG_DIM:hl * AUG_DIM + HEAD_DIM] = kh.astype(BF16)
        k_ref[0, :, hl * AUG_DIM + HEAD_DIM:(hl + 1) * AUG_DIM] = onehot
        for blk in range(t // MOBA_BLOCK):
            km_ref[0, blk, :, hl * HEAD_DIM:(hl + 1) * HEAD_DIM] = jnp.mean(
                kh[blk * MOBA_BLOCK:(blk + 1) * MOBA_BLOCK], axis=0, keepdims=True)


def _shared_kv(h, g, w_kv, k_g):
    bsz, seq, _ = h.shape
    t = KV_T
    n_groups = N_HEADS // HEAD_GROUP
    n_blocks = seq // MOBA_BLOCK
    return pl.pallas_call(
        _kv_kernel,
        grid=(bsz, seq // t, n_groups),
        in_specs=[
            pl.BlockSpec((1, t, D_MODEL), lambda b, s, n: (b, s, 0)),
            pl.BlockSpec((1, D_MODEL), lambda b, s, n: (0, 0)),
            pl.BlockSpec((D_MODEL, GROUP_COLS), lambda b, s, n: (0, n)),
            pl.BlockSpec((D_MODEL, GROUP_COLS), lambda b, s, n: (0, n + n_groups)),
            pl.BlockSpec((1, HEAD_DIM), lambda b, s, n: (0, 0)),
        ],
        out_specs=[
            pl.BlockSpec((1, t, HEAD_GROUP * AUG_DIM), lambda b, s, n: (b, s, n)),
            pl.BlockSpec((1, HEAD_GROUP * V_AUG_DIM, t), lambda b, s, n: (b, n, s)),
            pl.BlockSpec((1, t // MOBA_BLOCK, 1, GROUP_COLS), lambda b, s, n: (b, s, 0, n)),
        ],
        out_shape=[
            jax.ShapeDtypeStruct((bsz, seq, N_HEADS * AUG_DIM), BF16),
            jax.ShapeDtypeStruct((bsz, N_HEADS * V_AUG_DIM, seq), BF16),
            jax.ShapeDtypeStruct((bsz, n_blocks, 1, N_HEADS * HEAD_DIM), F32),
        ],
        scratch_shapes=[pltpu.VMEM((t, D_MODEL), BF16)],
        compiler_params=_params(3),
        name="shared_kv",
    )(h, g.reshape(1, D_MODEL), w_kv, w_kv, k_g.reshape(1, HEAD_DIM))


Q_T = 512
GATE_SLOTS = SUBLANES * 2


def _split_bf16(x):
    hi = x.astype(BF16)
    lo = (x - hi.astype(F32)).astype(BF16)
    return hi, lo


def _q_kernel(h_ref, g_ref, wqt_ref, qg_ref, km_ref, qt_ref):
    s = pl.program_id(1)
    xn = _rms_norm(h_ref[0], g_ref[...]).astype(BF16)
    projections = [_dot_nt(wqt_ref[n * GROUP_COLS:(n + 1) * GROUP_COLS, :], xn)
                   for n in range(N_HEADS // HEAD_GROUP)]
    for n, q_t in enumerate(projections):
        _q_group(s, q_t, qg_ref, km_ref[0, :, n * GROUP_COLS:(n + 1) * GROUP_COLS],
                 qt_ref.at[0, n * HEAD_GROUP * AUG_DIM:(n + 1) * HEAD_GROUP * AUG_DIM, :])


def _q_group(s, q_t, qg_ref, km, out_ref):
    t = Q_T
    qg = jnp.broadcast_to(qg_ref[...], (HEAD_DIM, t))
    heads = []
    for hl in range(HEAD_GROUP):
        x = q_t[hl * HEAD_DIM:(hl + 1) * HEAD_DIM]
        heads.append(x * lax.rsqrt(jnp.mean(x * x, axis=0, keepdims=True) + EPS) * qg)
    qn_t = jnp.concatenate(heads, axis=0)

    km_rows = jnp.concatenate([km] * HEAD_GROUP, axis=0)
    r_idx = lax.broadcasted_iota(jnp.int32, (HEAD_GROUP * GATE_SLOTS, GROUP_COLS), 0)
    c_idx = lax.broadcasted_iota(jnp.int32, (HEAD_GROUP * GATE_SLOTS, GROUP_COLS), 1)
    km_diag = jnp.where(r_idx // GATE_SLOTS == c_idx // HEAD_DIM, km_rows, 0.0)
    q_hi, q_lo = _split_bf16(qn_t)
    k_hi, k_lo = _split_bf16(km_diag)
    gate = _dot(k_hi, q_hi) + (_dot(k_lo, q_hi) + _dot(k_hi, q_lo))

    blk = lax.broadcasted_iota(jnp.int32, (GATE_SLOTS, t), 0)
    col = lax.broadcasted_iota(jnp.int32, (GATE_SLOTS, t), 1)
    own = s * (t // MOBA_BLOCK) + col // MOBA_BLOCK
    past = blk < own
    zeros_tail = jnp.zeros((AUG_DIM - HEAD_DIM - GATE_SLOTS, t), BF16)
    for hl in range(HEAD_GROUP):
        g = jnp.where(past, gate[hl * GATE_SLOTS:(hl + 1) * GATE_SLOTS], -jnp.inf)
        rank = jnp.zeros((GATE_SLOTS, t), jnp.int32)
        for other in range(GATE_SLOTS):
            go = jnp.broadcast_to(g[other:other + 1], (GATE_SLOTS, t))
            beats = (go > g) | ((go == g) & (blk > other))
            rank += jnp.where(beats, 1, 0)
        attend = ((rank < MOBA_TOPK) & past) | (blk >= own)
        bias = jnp.where(attend, 0.0, MASK_VALUE)
        base = hl * AUG_DIM
        out_ref[base:base + HEAD_DIM, :] = (heads[hl] * QK_SCALE).astype(BF16)
        out_ref[base + HEAD_DIM:base + HEAD_DIM + GATE_SLOTS, :] = bias.astype(BF16)
        out_ref[base + HEAD_DIM + GATE_SLOTS:base + AUG_DIM, :] = zeros_tail


def _queries(h, g, w_q_t, q_g, k_mean):
    bsz, seq, _ = h.shape
    t = Q_T
    n_blocks = seq // MOBA_BLOCK
    assert n_blocks == GATE_SLOTS
    return pl.pallas_call(
        _q_kernel,
        grid=(bsz, seq // t),
        in_specs=[
            pl.BlockSpec((1, t, D_MODEL), lambda b, s: (b, s, 0)),
            pl.BlockSpec((1, D_MODEL), lambda b, s: (0, 0)),
            pl.BlockSpec((N_HEADS * HEAD_DIM, D_MODEL), lambda b, s: (0, 0)),
            pl.BlockSpec((HEAD_DIM, 1), lambda b, s: (0, 0)),
            pl.BlockSpec((1, n_blocks, N_HEADS * HEAD_DIM), lambda b, s: (b, 0, 0)),
        ],
        out_specs=pl.BlockSpec((1, N_HEADS * AUG_DIM, t), lambda b, s: (b, 0, s)),
        out_shape=jax.ShapeDtypeStruct((bsz, N_HEADS * AUG_DIM, seq), BF16),
        compiler_params=_params(2),
        name="queries",
    )(h, g.reshape(1, D_MODEL), w_q_t, q_g.reshape(HEAD_DIM, 1),
      k_mean.reshape(bsz, n_blocks, N_HEADS * HEAD_DIM))


def _attn_kernel(qt_ref, k_ref, vt_ref, o_ref, m_ref, acc_ref):
    blk = MOBA_BLOCK
    n_blocks = qt_ref.shape[2] // blk
    k_pos = lax.broadcasted_iota(jnp.int32, (blk, blk), 0)
    q_pos = lax.broadcasted_iota(jnp.int32, (blk, blk), 1)
    causal = k_pos <= q_pos

    def rows(j):
        return slice(j * blk, (j + 1) * blk)

    own = [_dot(k_ref[0, rows(i), :], qt_ref[0, :, rows(i)]) for i in range(n_blocks)]
    for i in range(n_blocks):
        s = jnp.where(causal, own[i], MASK_VALUE)
        m = jnp.max(s, axis=0, keepdims=True)
        p = jnp.exp2(s - m)
        m_ref[:, rows(i)] = m
        acc_ref[:, rows(i)] = _dot(vt_ref[0, :, rows(i)], p.astype(BF16))

    def past_scores(j):
        return _dot(k_ref[0, rows(j), :], qt_ref[0, :, (j + 1) * blk:])

    s_next = past_scores(0)
    for j in range(n_blocks - 1):
        later = slice((j + 1) * blk, n_blocks * blk)
        s = s_next
        if j + 2 < n_blocks:
            s_next = past_scores(j + 1)
        m_old = m_ref[:, later]
        m_new = jnp.maximum(m_old, jnp.max(s, axis=0, keepdims=True))
        alpha = jnp.exp2(m_old - m_new)
        p = jnp.exp2(s - m_new)
        m_ref[:, later] = m_new
        acc_ref[:, later] = alpha * acc_ref[:, later] + _dot(vt_ref[0, :, rows(j)], p.astype(BF16))

    o = acc_ref[:HEAD_DIM, :] * (1.0 / acc_ref[HEAD_DIM:HEAD_DIM + 1, :])
    o_ref[0] = o.T.astype(BF16)


def _attention(q_aug_t, k_aug, v_aug_t):
    bsz, _, seq = v_aug_t.shape
    return pl.pallas_call(
        _attn_kernel,
        grid=(bsz, N_HEADS),
        in_specs=[
            pl.BlockSpec((1, AUG_DIM, seq), lambda b, h: (b, h, 0)),
            pl.BlockSpec((1, seq, AUG_DIM), lambda b, h: (b, 0, h)),
            pl.BlockSpec((1, V_AUG_DIM, seq), lambda b, h: (b, h, 0)),
        ],
        out_specs=pl.BlockSpec((1, seq, HEAD_DIM), lambda b, h: (b, 0, h)),
        out_shape=jax.ShapeDtypeStruct((bsz, seq, N_HEADS * HEAD_DIM), BF16),
        scratch_shapes=[
            pltpu.VMEM((1, seq), F32),
            pltpu.VMEM((V_AUG_DIM, seq), F32),
        ],
        compiler_params=_params(2),
        name="moba_attention",
    )(q_aug_t, k_aug, v_aug_t)


PROJ_TM = 1024
PROJ_TN = 1024


def _out_proj_kernel(o_ref, w_ref, x_ref, y_ref):
    y_ref[...] = x_ref[...] + _dot(o_ref[...], w_ref[...])


def _out_proj(o2d, w_o, x2d):
    m, k = o2d.shape
    n = w_o.shape[1]
    return pl.pallas_call(
        _out_proj_kernel,
        grid=(m // PROJ_TM, n // PROJ_TN),
        in_specs=[
            pl.BlockSpec((PROJ_TM, k), lambda i, j: (i, 0)),
            pl.BlockSpec((k, PROJ_TN), lambda i, j: (0, j)),
            pl.BlockSpec((PROJ_TM, PROJ_TN), lambda i, j: (i, j)),
        ],
        out_specs=pl.BlockSpec((PROJ_TM, PROJ_TN), lambda i, j: (i, j)),
        out_shape=jax.ShapeDtypeStruct((m, n), F32),
        compiler_params=_params(2),
        name="out_proj",
    )(o2d, w_o, x2d)


def kernel(x, norm_g, ffn_w_in, ffn_w_out, lru_w_in, lru_conv_w, lru_conv_b, lru_gate_w, lru_gate_b, lru_lambda, lru_w_out, kv_norm_g, w_kv, k_norm_g, attn_w_q, q_norm_g, attn_w_o):
    bsz, seq, d = x.shape
    rows = bsz * seq

    def ffn(h, layer, which):
        return _ffn(h.reshape(rows, d), norm_g[layer, 2 * which], ffn_w_in, ffn_w_out,
                    layer, which).reshape(bsz, seq, d)

    h = ffn(x, 0, 0)
    h = _lru(h, norm_g[0, 1], lru_w_in[0].astype(BF16), lru_conv_w[0], lru_conv_b[0],
             lru_gate_w[0].astype(BF16), lru_gate_b[0], lru_lambda[0], lru_w_out[0].astype(BF16))
    h = ffn(h, 0, 1)
    k_aug, v_t, k_mean = _shared_kv(h, kv_norm_g, w_kv.astype(BF16), k_norm_g)

    h = ffn(h, 1, 0)
    q_aug_t = _queries(h, norm_g[1, 1], attn_w_q[0].T.astype(BF16), q_norm_g[0], k_mean)
    o = _attention(q_aug_t, k_aug, v_t)
    h = _out_proj(o.reshape(rows, d), attn_w_o[0].astype(BF16), h.reshape(rows, d)).reshape(bsz, seq, d)
    h = ffn(h, 1, 1)
    return h
```

```python
import functools

import jax
import jax.numpy as jnp
from jax import lax
from jax.experimental import pallas as pl
from jax.experimental.pallas import tpu as pltpu

D_MODEL = 2048
D_FF = 5632
LRU_HEADS = 8
LRU_BLOCK = 256
LRU_WIDTH = LRU_HEADS * LRU_BLOCK
CONV_W = 4
LRU_C = 8.0
N_HEADS = 16
HEAD_DIM = 128
MOBA_BLOCK = 256
MOBA_TOPK = 3
EPS = 1e-6

F32 = jnp.float32
BF16 = jnp.bfloat16

SUBLANES = 8
LANES = 128
VMEM_LIMIT_BYTES = 56 * 1024 * 1024
MASK_VALUE = -1e30
LOG2_E = 1.4426950408889634
QK_SCALE = HEAD_DIM ** -0.5 * LOG2_E


def _params(n_axes):
    return pltpu.CompilerParams(
        dimension_semantics=("arbitrary",) * n_axes,
        vmem_limit_bytes=VMEM_LIMIT_BYTES,
    )


def _rms_norm(x, g):
    return x * lax.rsqrt(jnp.mean(x * x, axis=-1, keepdims=True) + EPS) * g


def _dot(a, b):
    return jnp.dot(a, b, preferred_element_type=F32)


def _dot_nt(a, b):
    return lax.dot_general(a, b, (((1,), (1,)), ((), ())), preferred_element_type=F32)


FFN_TM = 1024
FFN_TF = 256


def _ffn_kernel(x_ref, g_ref, wg_ref, wu_ref, wo_ref, o_ref, xn_ref):
    f = pl.program_id(1)

    @pl.when(f == 0)
    def _():
        x = x_ref[...]
        xn_ref[...] = _rms_norm(x, g_ref[...]).astype(BF16)
        o_ref[...] = x

    xn = xn_ref[...]
    gate = _dot(xn, wg_ref[...].astype(BF16))
    up = _dot(xn, wu_ref[...].astype(BF16))
    act = (0.5 * gate) * jax.nn.sigmoid(gate) * up
    o_ref[...] += _dot(act.astype(BF16), wo_ref[...].astype(BF16))


def _ffn(x2d, g, w_in, w_out, layer, which):
    m = x2d.shape[0]
    n_f = D_FF // FFN_TF
    return pl.pallas_call(
        _ffn_kernel,
        grid=(m // FFN_TM, n_f),
        in_specs=[
            pl.BlockSpec((FFN_TM, D_MODEL), lambda i, f: (i, 0)),
            pl.BlockSpec((1, D_MODEL), lambda i, f: (0, 0)),
            pl.BlockSpec((None, None, D_MODEL, FFN_TF), lambda i, f: (layer, which, 0, f)),
            pl.BlockSpec((None, None, D_MODEL, FFN_TF), lambda i, f: (layer, which, 0, f + n_f)),
            pl.BlockSpec((None, None, FFN_TF, D_MODEL), lambda i, f: (layer, which, f, 0)),
        ],
        out_specs=pl.BlockSpec((FFN_TM, D_MODEL), lambda i, f: (i, 0)),
        out_shape=jax.ShapeDtypeStruct((m, D_MODEL), F32),
        scratch_shapes=[pltpu.VMEM((FFN_TM, D_MODEL), BF16)],
        compiler_params=_params(2),
        name="ffn",
    )(x2d, g.reshape(1, D_MODEL), w_in, w_in, w_out)


LRU_T = 512


def _softplus(x):
    return jnp.maximum(x, 0.0) + jnp.log1p(jnp.exp(-jnp.abs(x)))


def _lru_kernel(h_ref, g_ref, win_ref, cw_ref, cb_ref, gw_ref, gb_ref, lam_ref, wo_ref,
                o_ref, carry_ref, halo_ref, hs_ref):
    s = pl.program_id(1)
    t, c = LRU_T, LRU_BLOCK

    @pl.when(s == 0)
    def _():
        carry_ref[...] = jnp.zeros(carry_ref.shape, F32)
        halo_ref[...] = jnp.zeros(halo_ref.shape, F32)

    x = h_ref[0]
    xn = _rms_norm(x, g_ref[...]).astype(BF16)
    o_ref[0] = x

    def cols(hd):
        return slice(hd * c, (hd + 1) * c)

    def front(hd):
        xb = _dot(xn, win_ref[:, cols(hd)])
        yb = _dot(xn, win_ref[:, cols(hd + LRU_HEADS)])
        ext = jnp.concatenate([halo_ref[hd], xb], axis=0)
        halo_ref[hd] = xb[t - SUBLANES:]
        cw = cw_ref[:, cols(hd)]
        xc = jnp.broadcast_to(cb_ref[:, cols(hd)], (t, c))
        for tap in range(CONV_W):
            back = CONV_W - 1 - tap
            shifted = ext if back == 0 else pltpu.roll(ext, back, 0)
            xc = xc + shifted[SUBLANES:] * cw[tap:tap + 1]
        xcb = xc.astype(BF16)
        return xc, yb, _dot(xcb, gw_ref[0, hd]), _dot(xcb, gw_ref[1, hd])

    def back(hd, xc, yb, r_pre, i_pre):
        gb = gb_ref[:, cols(hd)]
        r = jax.nn.sigmoid(r_pre + gb[0:1])
        i = jax.nn.sigmoid(i_pre + gb[1:2])
        log_a = (-LRU_C * r) * _softplus(-lam_ref[:, cols(hd)])
        a = jnp.exp(log_a)
        u = xc * i * jnp.exp2(0.5 * jnp.log2(-jnp.tanh(log_a) * (a * a + 1.0)))

        row_in_group = lax.broadcasted_iota(jnp.int32, (t, c), 0) & (SUBLANES - 1)
        d = 1
        while d < SUBLANES:
            keep = row_in_group >= d
            a_prev = jnp.where(keep, pltpu.roll(a, d, 0), 1.0)
            u_prev = jnp.where(keep, pltpu.roll(u, d, 0), 0.0)
            u = a * u_prev + u
            a = a * a_prev
            d *= 2
        prev = carry_ref[hd]
        for grp in range(t // SUBLANES):
            rows = slice(grp * SUBLANES, (grp + 1) * SUBLANES)
            hg = u[rows] + a[rows] * prev
            hs_ref[rows, :] = hg
            prev = jnp.broadcast_to(hg[SUBLANES - 1:SUBLANES], (SUBLANES, c))
        carry_ref[hd] = prev

        mixed = hs_ref[...] * jax.nn.gelu(yb, approximate=True)
        o_ref[0] += _dot(mixed.astype(BF16), wo_ref[cols(hd), :])

    pending = front(0)
    for hd in range(LRU_HEADS):
        current = pending
        if hd + 1 < LRU_HEADS:
            pending = front(hd + 1)
        back(hd, *current)


def _lru(h, g, w_in, conv_w, conv_b, gate_w, gate_b, lam, w_out):
    bsz, seq, _ = h.shape
    t, c = LRU_T, LRU_BLOCK

    def resident(shape):
        return pl.BlockSpec(shape, lambda b, s: (0,) * len(shape), pipeline_mode=pl.Buffered(1))

    return pl.pallas_call(
        _lru_kernel,
        grid=(bsz, seq // t),
        in_specs=[
            pl.BlockSpec((1, t, D_MODEL), lambda b, s: (b, s, 0)),
            resident((1, D_MODEL)),
            resident(w_in.shape),
            resident(conv_w.shape),
            resident((1, LRU_WIDTH)),
            resident(gate_w.shape),
            resident(gate_b.shape),
            resident((1, LRU_WIDTH)),
            resident(w_out.shape),
        ],
        out_specs=pl.BlockSpec((1, t, D_MODEL), lambda b, s: (b, s, 0)),
        out_shape=jax.ShapeDtypeStruct(h.shape, F32),
        scratch_shapes=[
            pltpu.VMEM((LRU_HEADS, SUBLANES, c), F32),
            pltpu.VMEM((LRU_HEADS, SUBLANES, c), F32),
            pltpu.VMEM((t, c), F32),
        ],
        compiler_params=_params(2),
        name="lru",
    )(h, g.reshape(1, D_MODEL), w_in, conv_w, conv_b.reshape(1, -1), gate_w, gate_b,
      lam.reshape(1, -1), w_out)


KV_T = 512
HEAD_GROUP = 8
GROUP_COLS = HEAD_GROUP * HEAD_DIM
AUG_DIM = 2 * HEAD_DIM
V_AUG_DIM = HEAD_DIM + 2 * SUBLANES


def _head_rms_norm(x, g):
    return x * lax.rsqrt(jnp.mean(x * x, axis=-1, keepdims=True) + EPS) * g


def _kv_kernel(h_ref, g_ref, wkv_ref, kg_ref, k_ref, vt_ref, km_ref):
    s = pl.program_id(1)
    t = KV_T
    xn = _rms_norm(h_ref[0], g_ref[...]).astype(BF16)
    n_groups = N_HEADS // HEAD_GROUP
    k_cols = N_HEADS * HEAD_DIM

    ones_rows = jnp.where(lax.broadcasted_iota(jnp.int32, (V_AUG_DIM - HEAD_DIM, t), 0) == 0,
                          1.0, 0.0).astype(BF16)
    row = lax.broadcasted_iota(jnp.int32, (t, LANES), 0)
    lane = lax.broadcasted_iota(jnp.int32, (t, LANES), 1)
    block_of_row = s * (t // MOBA_BLOCK) + row // MOBA_BLOCK
    onehot = jnp.where(lane == block_of_row, 1.0, 0.0).astype(BF16)

    for n in range(n_groups):
        k = _dot(xn, wkv_ref[:, n * GROUP_COLS:(n + 1) * GROUP_COLS])
        v = _dot(xn, wkv_ref[:, k_cols + n * GROUP_COLS:k_cols + (n + 1) * GROUP_COLS])
        v_t = v.T.astype(BF16)
        for hl in range(HEAD_GROUP):
            head = n * HEAD_GROUP + hl
            vt_ref[0, head * V_AUG_DIM:head * V_AUG_DIM + HEAD_DIM, :] = (
                v_t[hl * HEAD_DIM:(hl + 1) * HEAD_DIM])
            vt_ref[0, head * V_AUG_DIM + HEAD_DIM:(head + 1) * V_AUG_DIM, :] = ones_rows
        for hl in range(HEAD_GROUP):
            head = n * HEAD_GROUP + hl
            kh = _head_rms_norm(k[:, hl * HEAD_DIM:(hl + 1) * HEAD_DIM], kg_ref[...])
            k_ref[0, :, head * AUG_DIM:head * AUG_DIM + HEAD_DIM] = kh.astype(BF16)
            k_ref[0, :, head * AUG_DIM + HEAD_DIM:(head + 1) * AUG_DIM] = onehot
            for blk in range(t // MOBA_BLOCK):
                km_ref[0, blk, :, head * HEAD_DIM:(head + 1) * HEAD_DIM] = jnp.mean(
                    kh[blk * MOBA_BLOCK:(blk + 1) * MOBA_BLOCK], axis=0, keepdims=True)


def _resident(shape):
    return pl.BlockSpec(shape, lambda b, s: (0,) * len(shape), pipeline_mode=pl.Buffered(1))


def _shared_kv(h, g, w_kv, k_g):
    bsz, seq, _ = h.shape
    t = KV_T
    n_blocks = seq // MOBA_BLOCK
    return pl.pallas_call(
        _kv_kernel,
        grid=(bsz, seq // t),
        in_specs=[
            pl.BlockSpec((1, t, D_MODEL), lambda b, s: (b, s, 0)),
            _resident((1, D_MODEL)),
            _resident(w_kv.shape),
            _resident((1, HEAD_DIM)),
        ],
        out_specs=[
            pl.BlockSpec((1, t, N_HEADS * AUG_DIM), lambda b, s: (b, s, 0)),
            pl.BlockSpec((1, N_HEADS * V_AUG_DIM, t), lambda b, s: (b, 0, s)),
            pl.BlockSpec((1, t // MOBA_BLOCK, 1, N_HEADS * HEAD_DIM), lambda b, s: (b, s, 0, 0)),
        ],
        out_shape=[
            jax.ShapeDtypeStruct((bsz, seq, N_HEADS * AUG_DIM), BF16),
            jax.ShapeDtypeStruct((bsz, N_HEADS * V_AUG_DIM, seq), BF16),
            jax.ShapeDtypeStruct((bsz, n_blocks, 1, N_HEADS * HEAD_DIM), F32),
        ],
        compiler_params=_params(2),
        name="shared_kv",
    )(h, g.reshape(1, D_MODEL), w_kv, k_g.reshape(1, HEAD_DIM))


Q_T = 512
GATE_SLOTS = SUBLANES * 2


def _split_bf16(x):
    hi = x.astype(BF16)
    lo = (x - hi.astype(F32)).astype(BF16)
    return hi, lo


def _q_kernel(h_ref, g_ref, wqt_ref, qg_ref, km_ref, qt_ref):
    s = pl.program_id(1)
    xn = _rms_norm(h_ref[0], g_ref[...]).astype(BF16)
    projections = [_dot_nt(wqt_ref[n * GROUP_COLS:(n + 1) * GROUP_COLS, :], xn)
                   for n in range(N_HEADS // HEAD_GROUP)]
    for n, q_t in enumerate(projections):
        _q_group(s, q_t, qg_ref, km_ref[0, :, n * GROUP_COLS:(n + 1) * GROUP_COLS],
                 qt_ref.at[0, n * HEAD_GROUP * AUG_DIM:(n + 1) * HEAD_GROUP * AUG_DIM, :])


def _q_group(s, q_t, qg_ref, km, out_ref):
    t = Q_T
    qg = jnp.broadcast_to(qg_ref[...], (HEAD_DIM, t))
    heads = []
    for hl in range(HEAD_GROUP):
        x = q_t[hl * HEAD_DIM:(hl + 1) * HEAD_DIM]
        heads.append(x * lax.rsqrt(jnp.mean(x * x, axis=0, keepdims=True) + EPS) * qg)
    qn_t = jnp.concatenate(heads, axis=0)

    km_rows = jnp.concatenate([km] * HEAD_GROUP, axis=0)
    r_idx = lax.broadcasted_iota(jnp.int32, (HEAD_GROUP * GATE_SLOTS, GROUP_COLS), 0)
    c_idx = lax.broadcasted_iota(jnp.int32, (HEAD_GROUP * GATE_SLOTS, GROUP_COLS), 1)
    km_diag = jnp.where(r_idx // GATE_SLOTS == c_idx // HEAD_DIM, km_rows, 0.0)
    q_hi, q_lo = _split_bf16(qn_t)
    k_hi, k_lo = _split_bf16(km_diag)
    gate = _dot(k_hi, q_hi) + (_dot(k_lo, q_hi) + _dot(k_hi, q_lo))

    blk = lax.broadcasted_iota(jnp.int32, (GATE_SLOTS, t), 0)
    col = lax.broadcasted_iota(jnp.int32, (GATE_SLOTS, t), 1)
    own = s * (t // MOBA_BLOCK) + col // MOBA_BLOCK
    past = blk < own
    zeros_tail = jnp.zeros((AUG_DIM - HEAD_DIM - GATE_SLOTS, t), BF16)
    for hl in range(HEAD_GROUP):
        g = jnp.where(past, gate[hl * GATE_SLOTS:(hl + 1) * GATE_SLOTS], -jnp.inf)
        rank = jnp.zeros((GATE_SLOTS, t), jnp.int32)
        for other in range(GATE_SLOTS):
            go = jnp.broadcast_to(g[other:other + 1], (GATE_SLOTS, t))
            beats = (go > g) | ((go == g) & (blk > other))
            rank += jnp.where(beats, 1, 0)
        attend = ((rank < MOBA_TOPK) & past) | (blk >= own)
        bias = jnp.where(attend, 0.0, MASK_VALUE)
        base = hl * AUG_DIM
        out_ref[base:base + HEAD_DIM, :] = (heads[hl] * QK_SCALE).astype(BF16)
        out_ref[base + HEAD_DIM:base + HEAD_DIM + GATE_SLOTS, :] = bias.astype(BF16)
        out_ref[base + HEAD_DIM + GATE_SLOTS:base + AUG_DIM, :] = zeros_tail


def _queries(h, g, w_q_t, q_g, k_mean):
    bsz, seq, _ = h.shape
    t = Q_T
    n_blocks = seq // MOBA_BLOCK
    assert n_blocks == GATE_SLOTS
    return pl.pallas_call(
        _q_kernel,
        grid=(bsz, seq // t),
        in_specs=[
            pl.BlockSpec((1, t, D_MODEL), lambda b, s: (b, s, 0)),
            pl.BlockSpec((1, D_MODEL), lambda b, s: (0, 0)),
            pl.BlockSpec((N_HEADS * HEAD_DIM, D_MODEL), lambda b, s: (0, 0)),
            pl.BlockSpec((HEAD_DIM, 1), lambda b, s: (0, 0)),
            pl.BlockSpec((1, n_blocks, N_HEADS * HEAD_DIM), lambda b, s: (b, 0, 0)),
        ],
        out_specs=pl.BlockSpec((1, N_HEADS * AUG_DIM, t), lambda b, s: (b, 0, s)),
        out_shape=jax.ShapeDtypeStruct((bsz, N_HEADS * AUG_DIM, seq), BF16),
        compiler_params=_params(2),
        name="queries",
    )(h, g.reshape(1, D_MODEL), w_q_t, q_g.reshape(HEAD_DIM, 1),
      k_mean.reshape(bsz, n_blocks, N_HEADS * HEAD_DIM))


def _attn_kernel(qt_ref, k_ref, vt_ref, o_ref, m_ref, acc_ref):
    blk = MOBA_BLOCK
    n_blocks = qt_ref.shape[2] // blk
    k_pos = lax.broadcasted_iota(jnp.int32, (blk, blk), 0)
    q_pos = lax.broadcasted_iota(jnp.int32, (blk, blk), 1)
    causal = k_pos <= q_pos

    def rows(j):
        return slice(j * blk, (j + 1) * blk)

    own = [_dot(k_ref[0, rows(i), :], qt_ref[0, :, rows(i)]) for i in range(n_blocks)]
    for i in range(n_blocks):
        s = jnp.where(causal, own[i], MASK_VALUE)
        m = jnp.max(s, axis=0, keepdims=True)
        p = jnp.exp2(s - m)
        m_ref[:, rows(i)] = m
        acc_ref[:, rows(i)] = _dot(vt_ref[0, :, rows(i)], p.astype(BF16))

    def past_scores(j):
        return _dot(k_ref[0, rows(j), :], qt_ref[0, :, (j + 1) * blk:])

    s_next = past_scores(0)
    for j in range(n_blocks - 1):
        later = slice((j + 1) * blk, n_blocks * blk)
        s = s_next
        if j + 2 < n_blocks:
            s_next = past_scores(j + 1)
        m_old = m_ref[:, later]
        m_new = jnp.maximum(m_old, jnp.max(s, axis=0, keepdims=True))
        alpha = jnp.exp2(m_old - m_new)
        p = jnp.exp2(s - m_new)
        m_ref[:, later] = m_new
        acc_ref[:, later] = alpha * acc_ref[:, later] + _dot(vt_ref[0, :, rows(j)], p.astype(BF16))

    o = acc_ref[:HEAD_DIM, :] * (1.0 / acc_ref[HEAD_DIM:HEAD_DIM + 1, :])
    o_ref[0] = o.T.astype(BF16)


def _attention(q_aug_t, k_aug, v_aug_t):
    bsz, _, seq = v_aug_t.shape
    return pl.pallas_call(
        _attn_kernel,
        grid=(bsz, N_HEADS),
        in_specs=[
            pl.BlockSpec((1, AUG_DIM, seq), lambda b, h: (b, h, 0)),
            pl.BlockSpec((1, seq, AUG_DIM), lambda b, h: (b, 0, h)),
            pl.BlockSpec((1, V_AUG_DIM, seq), lambda b, h: (b, h, 0)),
        ],
        out_specs=pl.BlockSpec((1, seq, HEAD_DIM), lambda b, h: (b, 0, h)),
        out_shape=jax.ShapeDtypeStruct((bsz, seq, N_HEADS * HEAD_DIM), BF16),
        scratch_shapes=[
            pltpu.VMEM((1, seq), F32),
            pltpu.VMEM((V_AUG_DIM, seq), F32),
        ],
        compiler_params=_params(2),
        name="moba_attention",
    )(q_aug_t, k_aug, v_aug_t)


PROJ_TM = 512


def _out_proj_kernel(o_ref, w_ref, x_ref, y_ref):
    y_ref[...] = x_ref[...] + _dot(o_ref[...], w_ref[...])


def _out_proj(o2d, w_o, x2d):
    m, k = o2d.shape
    n = w_o.shape[1]
    return pl.pallas_call(
        _out_proj_kernel,
        grid=(m // PROJ_TM,),
        in_specs=[
            pl.BlockSpec((PROJ_TM, k), lambda i: (i, 0)),
            pl.BlockSpec((k, n), lambda i: (0, 0), pipeline_mode=pl.Buffered(1)),
            pl.BlockSpec((PROJ_TM, n), lambda i: (i, 0)),
        ],
        out_specs=pl.BlockSpec((PROJ_TM, n), lambda i: (i, 0)),
        out_shape=jax.ShapeDtypeStruct((m, n), F32),
        compiler_params=_params(1),
        name="out_proj",
    )(o2d, w_o, x2d)


def kernel(x, norm_g, ffn_w_in, ffn_w_out, lru_w_in, lru_conv_w, lru_conv_b, lru_gate_w, lru_gate_b, lru_lambda, lru_w_out, kv_norm_g, w_kv, k_norm_g, attn_w_q, q_norm_g, attn_w_o):
    bsz, seq, d = x.shape
    rows = bsz * seq

    def ffn(h, layer, which):
        return _ffn(h.reshape(rows, d), norm_g[layer, 2 * which], ffn_w_in, ffn_w_out,
                    layer, which).reshape(bsz, seq, d)

    h = ffn(x, 0, 0)
    h = _lru(h, norm_g[0, 1], lru_w_in[0].astype(BF16), lru_conv_w[0], lru_conv_b[0],
             lru_gate_w[0].astype(BF16), lru_gate_b[0], lru_lambda[0], lru_w_out[0].astype(BF16))
    h = ffn(h, 0, 1)
    k_aug, v_t, k_mean = _shared_kv(h, kv_norm_g, w_kv.astype(BF16), k_norm_g)

    h = ffn(h, 1, 0)
    q_aug_t = _queries(h, norm_g[1, 1], attn_w_q[0].T.astype(BF16), q_norm_g[0], k_mean)
    o = _attention(q_aug_t, k_aug, v_t)
    h = _out_proj(o.reshape(rows, d), attn_w_o[0].astype(BF16), h.reshape(rows, d)).reshape(bsz, seq, d)
    h = ffn(h, 1, 1)
    return h
```

```python
import jax
import jax.numpy as jnp
from jax import lax
from jax.experimental import pallas as pl
from jax.experimental.pallas import tpu as pltpu

D_MODEL = 2048
D_FF = 5632
LRU_HEADS = 8
LRU_BLOCK = 256
LRU_WIDTH = LRU_HEADS * LRU_BLOCK
CONV_W = 4
LRU_C = 8.0
N_HEADS = 16
HEAD_DIM = 128
MOBA_BLOCK = 256
MOBA_TOPK = 3
EPS = 1e-6

F32 = jnp.float32
BF16 = jnp.bfloat16

SUBLANES = 8
LANES = 128
VMEM_LIMIT_BYTES = 56 * 1024 * 1024
MASK_VALUE = -1e30
LOG2_E = 1.4426950408889634
QK_SCALE = HEAD_DIM ** -0.5 * LOG2_E


def _params(n_axes):
    return pltpu.CompilerParams(
        dimension_semantics=("arbitrary",) * n_axes,
        vmem_limit_bytes=VMEM_LIMIT_BYTES,
    )


def _resident(shape):
    return pl.BlockSpec(shape, lambda b, s: (0,) * len(shape), pipeline_mode=pl.Buffered(1))


def _rms_norm(x, g):
    return x * lax.rsqrt(jnp.mean(x * x, axis=-1, keepdims=True) + EPS) * g


def _dot(a, b):
    return jnp.dot(a, b, preferred_element_type=F32)


def _dot_nt(a, b):
    return lax.dot_general(a, b, (((1,), (1,)), ((), ())), preferred_element_type=F32)


FFN_TM = 1024
FFN_TF = 256


def _ffn_kernel(x_ref, g_ref, wg_ref, wu_ref, wo_ref, o_ref, xn_ref):
    f = pl.program_id(1)

    @pl.when(f == 0)
    def _():
        x = x_ref[...]
        xn_ref[...] = _rms_norm(x, g_ref[...]).astype(BF16)
        o_ref[...] = x

    xn = xn_ref[...]
    gate = _dot(xn, wg_ref[...].astype(BF16))
    up = _dot(xn, wu_ref[...].astype(BF16))
    act = (0.5 * gate) * jax.nn.sigmoid(gate) * up
    o_ref[...] += _dot(act.astype(BF16), wo_ref[...].astype(BF16))


def _ffn(x2d, g, w_in, w_out, layer, which):
    m = x2d.shape[0]
    n_f = D_FF // FFN_TF
    return pl.pallas_call(
        _ffn_kernel,
        grid=(m // FFN_TM, n_f),
        in_specs=[
            pl.BlockSpec((FFN_TM, D_MODEL), lambda i, f: (i, 0)),
            pl.BlockSpec((1, D_MODEL), lambda i, f: (0, 0)),
            pl.BlockSpec((None, None, D_MODEL, FFN_TF), lambda i, f: (layer, which, 0, f)),
            pl.BlockSpec((None, None, D_MODEL, FFN_TF), lambda i, f: (layer, which, 0, f + n_f)),
            pl.BlockSpec((None, None, FFN_TF, D_MODEL), lambda i, f: (layer, which, f, 0)),
        ],
        out_specs=pl.BlockSpec((FFN_TM, D_MODEL), lambda i, f: (i, 0)),
        out_shape=jax.ShapeDtypeStruct((m, D_MODEL), F32),
        scratch_shapes=[pltpu.VMEM((FFN_TM, D_MODEL), BF16)],
        compiler_params=_params(2),
        name="ffn",
    )(x2d, g.reshape(1, D_MODEL), w_in, w_in, w_out)


LRU_T = 512


def _softplus(x):
    return jnp.maximum(x, 0.0) + jnp.log1p(jnp.exp(-jnp.abs(x)))


def _lru_kernel(h_ref, g_ref, win_ref, cw_ref, cb_ref, gw_ref, gb_ref, lam_ref, wo_ref,
                o_ref, carry_ref, halo_ref, hs_ref):
    s = pl.program_id(1)
    t, c = LRU_T, LRU_BLOCK

    @pl.when(s == 0)
    def _():
        carry_ref[...] = jnp.zeros(carry_ref.shape, F32)
        halo_ref[...] = jnp.zeros(halo_ref.shape, F32)

    x = h_ref[0]
    xn = _rms_norm(x, g_ref[...]).astype(BF16)
    o_ref[0] = x

    def cols(hd):
        return slice(hd * c, (hd + 1) * c)

    def front(hd):
        xb = _dot(xn, win_ref[:, cols(hd)])
        yb = _dot(xn, win_ref[:, cols(hd + LRU_HEADS)])
        ext = jnp.concatenate([halo_ref[hd], xb], axis=0)
        halo_ref[hd] = xb[t - SUBLANES:]
        cw = cw_ref[:, cols(hd)]
        xc = jnp.broadcast_to(cb_ref[:, cols(hd)], (t, c))
        for tap in range(CONV_W):
            back = CONV_W - 1 - tap
            shifted = ext if back == 0 else pltpu.roll(ext, back, 0)
            xc = xc + shifted[SUBLANES:] * cw[tap:tap + 1]
        xcb = xc.astype(BF16)
        return xc, yb, _dot(xcb, gw_ref[0, hd]), _dot(xcb, gw_ref[1, hd])

    def back(hd, xc, yb, r_pre, i_pre):
        gb = gb_ref[:, cols(hd)]
        r = jax.nn.sigmoid(r_pre + gb[0:1])
        i = jax.nn.sigmoid(i_pre + gb[1:2])
        log_a = (-LRU_C * r) * _softplus(-lam_ref[:, cols(hd)])
        a = jnp.exp(log_a)
        u = xc * i * jnp.exp2(0.5 * jnp.log2(-jnp.tanh(log_a) * (a * a + 1.0)))

        row_in_group = lax.broadcasted_iota(jnp.int32, (t, c), 0) & (SUBLANES - 1)
        d = 1
        while d < SUBLANES:
            keep = row_in_group >= d
            a_prev = jnp.where(keep, pltpu.roll(a, d, 0), 1.0)
            u_prev = jnp.where(keep, pltpu.roll(u, d, 0), 0.0)
            u = a * u_prev + u
            a = a * a_prev
            d *= 2
        prev = carry_ref[hd]
        for grp in range(t // SUBLANES):
            rows = slice(grp * SUBLANES, (grp + 1) * SUBLANES)
            hg = u[rows] + a[rows] * prev
            hs_ref[rows, :] = hg
            prev = jnp.broadcast_to(hg[SUBLANES - 1:SUBLANES], (SUBLANES, c))
        carry_ref[hd] = prev

        mixed = hs_ref[...] * jax.nn.gelu(yb, approximate=True)
        o_ref[0] += _dot(mixed.astype(BF16), wo_ref[cols(hd), :])

    pending = front(0)
    for hd in range(LRU_HEADS):
        current = pending
        if hd + 1 < LRU_HEADS:
            pending = front(hd + 1)
        back(hd, *current)


def _lru(h, g, w_in, conv_w, conv_b, gate_w, gate_b, lam, w_out):
    bsz, seq, _ = h.shape
    t, c = LRU_T, LRU_BLOCK

    return pl.pallas_call(
        _lru_kernel,
        grid=(bsz, seq // t),
        in_specs=[
            pl.BlockSpec((1, t, D_MODEL), lambda b, s: (b, s, 0)),
            _resident((1, D_MODEL)),
            _resident(w_in.shape),
            _resident(conv_w.shape),
            _resident((1, LRU_WIDTH)),
            _resident(gate_w.shape),
            _resident(gate_b.shape),
            _resident((1, LRU_WIDTH)),
            _resident(w_out.shape),
        ],
        out_specs=pl.BlockSpec((1, t, D_MODEL), lambda b, s: (b, s, 0)),
        out_shape=jax.ShapeDtypeStruct(h.shape, F32),
        scratch_shapes=[
            pltpu.VMEM((LRU_HEADS, SUBLANES, c), F32),
            pltpu.VMEM((LRU_HEADS, SUBLANES, c), F32),
            pltpu.VMEM((t, c), F32),
        ],
        compiler_params=_params(2),
        name="lru",
    )(h, g.reshape(1, D_MODEL), w_in, conv_w, conv_b.reshape(1, -1), gate_w, gate_b,
      lam.reshape(1, -1), w_out)


KV_T = 512
HEAD_GROUP = 8
GROUP_COLS = HEAD_GROUP * HEAD_DIM
AUG_DIM = 2 * HEAD_DIM
V_AUG_DIM = HEAD_DIM + 2 * SUBLANES


def _head_rms_norm(x, g):
    return x * lax.rsqrt(jnp.mean(x * x, axis=-1, keepdims=True) + EPS) * g


def _kv_kernel(h_ref, g_ref, wkv_ref, kg_ref, k_ref, vt_ref, km_ref):
    s = pl.program_id(1)
    t = KV_T
    xn = _rms_norm(h_ref[0], g_ref[...]).astype(BF16)
    n_groups = N_HEADS // HEAD_GROUP
    k_cols = N_HEADS * HEAD_DIM

    ones_rows = jnp.where(lax.broadcasted_iota(jnp.int32, (V_AUG_DIM - HEAD_DIM, t), 0) == 0,
                          1.0, 0.0).astype(BF16)
    row = lax.broadcasted_iota(jnp.int32, (t, LANES), 0)
    lane = lax.broadcasted_iota(jnp.int32, (t, LANES), 1)
    block_of_row = s * (t // MOBA_BLOCK) + row // MOBA_BLOCK
    onehot = jnp.where(lane == block_of_row, 1.0, 0.0).astype(BF16)

    for n in range(n_groups):
        k = _dot(xn, wkv_ref[:, n * GROUP_COLS:(n + 1) * GROUP_COLS])
        v = _dot(xn, wkv_ref[:, k_cols + n * GROUP_COLS:k_cols + (n + 1) * GROUP_COLS])
        v_t = v.T.astype(BF16)
        for hl in range(HEAD_GROUP):
            head = n * HEAD_GROUP + hl
            vt_ref[0, head * V_AUG_DIM:head * V_AUG_DIM + HEAD_DIM, :] = (
                v_t[hl * HEAD_DIM:(hl + 1) * HEAD_DIM])
            vt_ref[0, head * V_AUG_DIM + HEAD_DIM:(head + 1) * V_AUG_DIM, :] = ones_rows
        for hl in range(HEAD_GROUP):
            head = n * HEAD_GROUP + hl
            kh = _head_rms_norm(k[:, hl * HEAD_DIM:(hl + 1) * HEAD_DIM], kg_ref[...])
            k_ref[0, :, head * AUG_DIM:head * AUG_DIM + HEAD_DIM] = kh.astype(BF16)
            k_ref[0, :, head * AUG_DIM + HEAD_DIM:(head + 1) * AUG_DIM] = onehot
            for blk in range(t // MOBA_BLOCK):
                km_ref[0, blk, :, head * HEAD_DIM:(head + 1) * HEAD_DIM] = jnp.mean(
                    kh[blk * MOBA_BLOCK:(blk + 1) * MOBA_BLOCK], axis=0, keepdims=True)


def _shared_kv(h, g, w_kv, k_g):
    bsz, seq, _ = h.shape
    t = KV_T
    n_blocks = seq // MOBA_BLOCK
    return pl.pallas_call(
        _kv_kernel,
        grid=(bsz, seq // t),
        in_specs=[
            pl.BlockSpec((1, t, D_MODEL), lambda b, s: (b, s, 0)),
            _resident((1, D_MODEL)),
            _resident(w_kv.shape),
            _resident((1, HEAD_DIM)),
        ],
        out_specs=[
            pl.BlockSpec((1, t, N_HEADS * AUG_DIM), lambda b, s: (b, s, 0)),
            pl.BlockSpec((1, N_HEADS * V_AUG_DIM, t), lambda b, s: (b, 0, s)),
            pl.BlockSpec((1, t // MOBA_BLOCK, 1, N_HEADS * HEAD_DIM), lambda b, s: (b, s, 0, 0)),
        ],
        out_shape=[
            jax.ShapeDtypeStruct((bsz, seq, N_HEADS * AUG_DIM), BF16),
            jax.ShapeDtypeStruct((bsz, N_HEADS * V_AUG_DIM, seq), BF16),
            jax.ShapeDtypeStruct((bsz, n_blocks, 1, N_HEADS * HEAD_DIM), F32),
        ],
        compiler_params=_params(2),
        name="shared_kv",
    )(h, g.reshape(1, D_MODEL), w_kv, k_g.reshape(1, HEAD_DIM))


Q_T = 512
GATE_SLOTS = SUBLANES * 2


def _split_bf16(x):
    hi = x.astype(BF16)
    lo = (x - hi.astype(F32)).astype(BF16)
    return hi, lo


def _q_kernel(h_ref, g_ref, wqt_ref, qg_ref, km_ref, qt_ref):
    s = pl.program_id(1)
    xn = _rms_norm(h_ref[0], g_ref[...]).astype(BF16)
    projections = [_dot_nt(wqt_ref[n * GROUP_COLS:(n + 1) * GROUP_COLS, :], xn)
                   for n in range(N_HEADS // HEAD_GROUP)]
    for n, q_t in enumerate(projections):
        _q_group(s, q_t, qg_ref, km_ref[0, :, n * GROUP_COLS:(n + 1) * GROUP_COLS],
                 qt_ref.at[0, n * HEAD_GROUP * AUG_DIM:(n + 1) * HEAD_GROUP * AUG_DIM, :])


def _q_group(s, q_t, qg_ref, km, out_ref):
    t = Q_T
    qg = jnp.broadcast_to(qg_ref[...], (HEAD_DIM, t))
    heads = []
    for hl in range(HEAD_GROUP):
        x = q_t[hl * HEAD_DIM:(hl + 1) * HEAD_DIM]
        heads.append(x * lax.rsqrt(jnp.mean(x * x, axis=0, keepdims=True) + EPS) * qg)
    qn_t = jnp.concatenate(heads, axis=0)

    km_rows = jnp.concatenate([km] * HEAD_GROUP, axis=0)
    r_idx = lax.broadcasted_iota(jnp.int32, (HEAD_GROUP * GATE_SLOTS, GROUP_COLS), 0)
    c_idx = lax.broadcasted_iota(jnp.int32, (HEAD_GROUP * GATE_SLOTS, GROUP_COLS), 1)
    km_diag = jnp.where(r_idx // GATE_SLOTS == c_idx // HEAD_DIM, km_rows, 0.0)
    q_hi, q_lo = _split_bf16(qn_t)
    k_hi, k_lo = _split_bf16(km_diag)
    gate = _dot(k_hi, q_hi) + (_dot(k_lo, q_hi) + _dot(k_hi, q_lo))

    blk = lax.broadcasted_iota(jnp.int32, (GATE_SLOTS, t), 0)
    col = lax.broadcasted_iota(jnp.int32, (GATE_SLOTS, t), 1)
    own = s * (t // MOBA_BLOCK) + col // MOBA_BLOCK
    past = blk < own
    zeros_tail = jnp.zeros((AUG_DIM - HEAD_DIM - GATE_SLOTS, t), BF16)
    for hl in range(HEAD_GROUP):
        g = jnp.where(past, gate[hl * GATE_SLOTS:(hl + 1) * GATE_SLOTS], -jnp.inf)
        selected = blk < 0
        for _ in range(MOBA_TOPK):
            best = jnp.max(g, axis=0, keepdims=True)
            first = jnp.min(jnp.where(g == best, blk, GATE_SLOTS), axis=0, keepdims=True)
            pick = blk == first
            selected = selected | pick
            g = jnp.where(pick, -jnp.inf, g)
        attend = (selected & past) | (blk >= own)
        bias = jnp.where(attend, 0.0, MASK_VALUE)
        base = hl * AUG_DIM
        out_ref[base:base + HEAD_DIM, :] = (heads[hl] * QK_SCALE).astype(BF16)
        out_ref[base + HEAD_DIM:base + HEAD_DIM + GATE_SLOTS, :] = bias.astype(BF16)
        out_ref[base + HEAD_DIM + GATE_SLOTS:base + AUG_DIM, :] = zeros_tail


def _queries(h, g, w_q_t, q_g, k_mean):
    bsz, seq, _ = h.shape
    t = Q_T
    n_blocks = seq // MOBA_BLOCK
    assert n_blocks == GATE_SLOTS
    return pl.pallas_call(
        _q_kernel,
        grid=(bsz, seq // t),
        in_specs=[
            pl.BlockSpec((1, t, D_MODEL), lambda b, s: (b, s, 0)),
            _resident((1, D_MODEL)),
            _resident((N_HEADS * HEAD_DIM, D_MODEL)),
            _resident((HEAD_DIM, 1)),
            pl.BlockSpec((1, n_blocks, N_HEADS * HEAD_DIM), lambda b, s: (b, 0, 0)),
        ],
        out_specs=pl.BlockSpec((1, N_HEADS * AUG_DIM, t), lambda b, s: (b, 0, s)),
        out_shape=jax.ShapeDtypeStruct((bsz, N_HEADS * AUG_DIM, seq), BF16),
        compiler_params=_params(2),
        name="queries",
    )(h, g.reshape(1, D_MODEL), w_q_t, q_g.reshape(HEAD_DIM, 1),
      k_mean.reshape(bsz, n_blocks, N_HEADS * HEAD_DIM))


def _attn_kernel(qt_ref, k_ref, vt_ref, o_ref, m_ref, acc_ref):
    blk = MOBA_BLOCK
    n_blocks = qt_ref.shape[2] // blk
    k_pos = lax.broadcasted_iota(jnp.int32, (blk, blk), 0)
    q_pos = lax.broadcasted_iota(jnp.int32, (blk, blk), 1)
    causal = k_pos <= q_pos

    def rows(j):
        return slice(j * blk, (j + 1) * blk)

    own = [_dot(k_ref[0, rows(i), :], qt_ref[0, :, rows(i)]) for i in range(n_blocks)]
    for i in range(n_blocks):
        s = jnp.where(causal, own[i], MASK_VALUE)
        m = jnp.max(s, axis=0, keepdims=True)
        p = jnp.exp2(s - m)
        m_ref[:, rows(i)] = m
        acc_ref[:, rows(i)] = _dot(vt_ref[0, :, rows(i)], p.astype(BF16))

    def past_scores(j):
        return _dot(k_ref[0, rows(j), :], qt_ref[0, :, (j + 1) * blk:])

    s_next = past_scores(0)
    for j in range(n_blocks - 1):
        later = slice((j + 1) * blk, n_blocks * blk)
        s = s_next
        if j + 2 < n_blocks:
            s_next = past_scores(j + 1)
        m_old = m_ref[:, later]
        m_new = jnp.maximum(m_old, jnp.max(s, axis=0, keepdims=True))
        alpha = jnp.exp2(m_old - m_new)
        p = jnp.exp2(s - m_new)
        m_ref[:, later] = m_new
        acc_ref[:, later] = alpha * acc_ref[:, later] + _dot(vt_ref[0, :, rows(j)], p.astype(BF16))

    o = acc_ref[:HEAD_DIM, :] * (1.0 / acc_ref[HEAD_DIM:HEAD_DIM + 1, :])
    o_ref[0] = o.T.astype(BF16)


def _attention(q_aug_t, k_aug, v_aug_t):
    bsz, _, seq = v_aug_t.shape
    return pl.pallas_call(
        _attn_kernel,
        grid=(bsz, N_HEADS),
        in_specs=[
            pl.BlockSpec((1, AUG_DIM, seq), lambda b, h: (b, h, 0)),
            pl.BlockSpec((1, seq, AUG_DIM), lambda b, h: (b, 0, h)),
            pl.BlockSpec((1, V_AUG_DIM, seq), lambda b, h: (b, h, 0)),
        ],
        out_specs=pl.BlockSpec((1, seq, HEAD_DIM), lambda b, h: (b, 0, h)),
        out_shape=jax.ShapeDtypeStruct((bsz, seq, N_HEADS * HEAD_DIM), BF16),
        scratch_shapes=[
            pltpu.VMEM((1, seq), F32),
            pltpu.VMEM((V_AUG_DIM, seq), F32),
        ],
        compiler_params=_params(2),
        name="moba_attention",
    )(q_aug_t, k_aug, v_aug_t)


PROJ_TM = 512


def _out_proj_kernel(o_ref, w_ref, x_ref, y_ref):
    y_ref[...] = x_ref[...] + _dot(o_ref[...], w_ref[...])


def _out_proj(o2d, w_o, x2d):
    m, k = o2d.shape
    n = w_o.shape[1]
    return pl.pallas_call(
        _out_proj_kernel,
        grid=(m // PROJ_TM,),
        in_specs=[
            pl.BlockSpec((PROJ_TM, k), lambda i: (i, 0)),
            pl.BlockSpec((k, n), lambda i: (0, 0), pipeline_mode=pl.Buffered(1)),
            pl.BlockSpec((PROJ_TM, n), lambda i: (i, 0)),
        ],
        out_specs=pl.BlockSpec((PROJ_TM, n), lambda i: (i, 0)),
        out_shape=jax.ShapeDtypeStruct((m, n), F32),
        compiler_params=_params(1),
        name="out_proj",
    )(o2d, w_o, x2d)


def kernel(x, norm_g, ffn_w_in, ffn_w_out, lru_w_in, lru_conv_w, lru_conv_b, lru_gate_w, lru_gate_b, lru_lambda, lru_w_out, kv_norm_g, w_kv, k_norm_g, attn_w_q, q_norm_g, attn_w_o):
    bsz, seq, d = x.shape
    rows = bsz * seq

    def ffn(h, layer, which):
        return _ffn(h.reshape(rows, d), norm_g[layer, 2 * which], ffn_w_in, ffn_w_out,
                    layer, which).reshape(bsz, seq, d)

    h = ffn(x, 0, 0)
    h = _lru(h, norm_g[0, 1], lru_w_in[0].astype(BF16), lru_conv_w[0], lru_conv_b[0],
             lru_gate_w[0].astype(BF16), lru_gate_b[0], lru_lambda[0], lru_w_out[0].astype(BF16))
    h = ffn(h, 0, 1)
    k_aug, v_t, k_mean = _shared_kv(h, kv_norm_g, w_kv.astype(BF16), k_norm_g)

    h = ffn(h, 1, 0)
    q_aug_t = _queries(h, norm_g[1, 1], attn_w_q[0].T.astype(BF16), q_norm_g[0], k_mean)
    o = _attention(q_aug_t, k_aug, v_t)
    h = _out_proj(o.reshape(rows, d), attn_w_o[0].astype(BF16), h.reshape(rows, d)).reshape(bsz, seq, d)
    h = ffn(h, 1, 1)
    return h
```

```python
import jax
import jax.numpy as jnp
from jax import lax
from jax.experimental import pallas as pl
from jax.experimental.pallas import tpu as pltpu

D_MODEL = 2048
D_FF = 5632
LRU_HEADS = 8
LRU_BLOCK = 256
LRU_WIDTH = LRU_HEADS * LRU_BLOCK
CONV_W = 4
LRU_C = 8.0
N_HEADS = 16
HEAD_DIM = 128
MOBA_BLOCK = 256
MOBA_TOPK = 3
EPS = 1e-6

F32 = jnp.float32
BF16 = jnp.bfloat16

SUBLANES = 8
LANES = 128
VMEM_LIMIT_BYTES = 56 * 1024 * 1024
MASK_VALUE = -1e30
LOG2_E = 1.4426950408889634
QK_SCALE = HEAD_DIM ** -0.5 * LOG2_E


def _params(n_axes):
    return pltpu.CompilerParams(
        dimension_semantics=("arbitrary",) * n_axes,
        vmem_limit_bytes=VMEM_LIMIT_BYTES,
    )


def _resident(shape):
    return pl.BlockSpec(shape, lambda b, s: (0,) * len(shape), pipeline_mode=pl.Buffered(1))


def _rms_norm(x, g):
    return x * lax.rsqrt(jnp.mean(x * x, axis=-1, keepdims=True) + EPS) * g


def _dot(a, b):
    return jnp.dot(a, b, preferred_element_type=F32)


def _dot_nt(a, b):
    return lax.dot_general(a, b, (((1,), (1,)), ((), ())), preferred_element_type=F32)


FFN_TM = 1024
FFN_TF = 256


def _ffn_kernel(x_ref, g_ref, wg_ref, wu_ref, wo_ref, o_ref, xn_ref):
    f = pl.program_id(1)

    @pl.when(f == 0)
    def _():
        x = x_ref[...]
        xn_ref[...] = _rms_norm(x, g_ref[...]).astype(BF16)
        o_ref[...] = x

    xn = xn_ref[...]
    gate = _dot(xn, wg_ref[...].astype(BF16))
    up = _dot(xn, wu_ref[...].astype(BF16))
    act = (0.5 * gate) * jax.nn.sigmoid(gate) * up
    o_ref[...] += _dot(act.astype(BF16), wo_ref[...].astype(BF16))


def _ffn(x2d, g, w_in, w_out, layer, which):
    m = x2d.shape[0]
    n_f = D_FF // FFN_TF
    return pl.pallas_call(
        _ffn_kernel,
        grid=(m // FFN_TM, n_f),
        in_specs=[
            pl.BlockSpec((FFN_TM, D_MODEL), lambda i, f: (i, 0)),
            pl.BlockSpec((1, D_MODEL), lambda i, f: (0, 0)),
            pl.BlockSpec((None, None, D_MODEL, FFN_TF), lambda i, f: (layer, which, 0, f)),
            pl.BlockSpec((None, None, D_MODEL, FFN_TF), lambda i, f: (layer, which, 0, f + n_f)),
            pl.BlockSpec((None, None, FFN_TF, D_MODEL), lambda i, f: (layer, which, f, 0)),
        ],
        out_specs=pl.BlockSpec((FFN_TM, D_MODEL), lambda i, f: (i, 0)),
        out_shape=jax.ShapeDtypeStruct((m, D_MODEL), F32),
        scratch_shapes=[pltpu.VMEM((FFN_TM, D_MODEL), BF16)],
        compiler_params=_params(2),
        name="ffn",
    )(x2d, g.reshape(1, D_MODEL), w_in, w_in, w_out)


LRU_T = 512


def _softplus(x):
    return jnp.maximum(x, 0.0) + jnp.log1p(jnp.exp(-jnp.abs(x)))


def _lru_kernel(h_ref, g_ref, win_ref, cw_ref, cb_ref, gw_ref, gb_ref, lam_ref, wo_ref,
                o_ref, carry_ref, halo_ref, hs_ref, mixed_ref):
    s = pl.program_id(1)
    t, c = LRU_T, LRU_BLOCK

    @pl.when(s == 0)
    def _():
        carry_ref[...] = jnp.zeros(carry_ref.shape, F32)
        halo_ref[...] = jnp.zeros(halo_ref.shape, F32)

    x = h_ref[0]
    xn = _rms_norm(x, g_ref[...]).astype(BF16)

    def cols(hd):
        return slice(hd * c, (hd + 1) * c)

    def front(hd):
        xb = _dot(xn, win_ref[:, cols(hd)])
        yb = _dot(xn, win_ref[:, cols(hd + LRU_HEADS)])
        ext = jnp.concatenate([halo_ref[hd], xb], axis=0)
        halo_ref[hd] = xb[t - SUBLANES:]
        cw = cw_ref[:, cols(hd)]
        xc = jnp.broadcast_to(cb_ref[:, cols(hd)], (t, c))
        for tap in range(CONV_W):
            back = CONV_W - 1 - tap
            shifted = ext if back == 0 else pltpu.roll(ext, back, 0)
            xc = xc + shifted[SUBLANES:] * cw[tap:tap + 1]
        xcb = xc.astype(BF16)
        return xc, yb, _dot(xcb, gw_ref[0, hd]), _dot(xcb, gw_ref[1, hd])

    def back(hd, xc, yb, r_pre, i_pre):
        gb = gb_ref[:, cols(hd)]
        r = jax.nn.sigmoid(r_pre + gb[0:1])
        i = jax.nn.sigmoid(i_pre + gb[1:2])
        log_a = (-LRU_C * r) * _softplus(-lam_ref[:, cols(hd)])
        a = jnp.exp(log_a)
        u = xc * i * jnp.exp2(0.5 * jnp.log2(-jnp.tanh(log_a) * (a * a + 1.0)))

        row_in_group = lax.broadcasted_iota(jnp.int32, (t, c), 0) & (SUBLANES - 1)
        d = 1
        while d < SUBLANES:
            keep = row_in_group >= d
            a_prev = jnp.where(keep, pltpu.roll(a, d, 0), 1.0)
            u_prev = jnp.where(keep, pltpu.roll(u, d, 0), 0.0)
            u = a * u_prev + u
            a = a * a_prev
            d *= 2
        prev = carry_ref[hd]
        for grp in range(t // SUBLANES):
            rows = slice(grp * SUBLANES, (grp + 1) * SUBLANES)
            hg = u[rows] + a[rows] * prev
            hs_ref[rows, :] = hg
            prev = jnp.broadcast_to(hg[SUBLANES - 1:SUBLANES], (SUBLANES, c))
        carry_ref[hd] = prev

        mixed = hs_ref[...] * jax.nn.gelu(yb, approximate=True)
        mixed_ref[:, cols(hd)] = mixed.astype(BF16)

    pending = front(0)
    for hd in range(LRU_HEADS):
        current = pending
        if hd + 1 < LRU_HEADS:
            pending = front(hd + 1)
        back(hd, *current)
    o_ref[0] = x + _dot(mixed_ref[...], wo_ref[...])


def _lru(h, g, w_in, conv_w, conv_b, gate_w, gate_b, lam, w_out):
    bsz, seq, _ = h.shape
    t, c = LRU_T, LRU_BLOCK

    return pl.pallas_call(
        _lru_kernel,
        grid=(bsz, seq // t),
        in_specs=[
            pl.BlockSpec((1, t, D_MODEL), lambda b, s: (b, s, 0)),
            _resident((1, D_MODEL)),
            _resident(w_in.shape),
            _resident(conv_w.shape),
            _resident((1, LRU_WIDTH)),
            _resident(gate_w.shape),
            _resident(gate_b.shape),
            _resident((1, LRU_WIDTH)),
            _resident(w_out.shape),
        ],
        out_specs=pl.BlockSpec((1, t, D_MODEL), lambda b, s: (b, s, 0)),
        out_shape=jax.ShapeDtypeStruct(h.shape, F32),
        scratch_shapes=[
            pltpu.VMEM((LRU_HEADS, SUBLANES, c), F32),
            pltpu.VMEM((LRU_HEADS, SUBLANES, c), F32),
            pltpu.VMEM((t, c), F32),
            pltpu.VMEM((t, LRU_WIDTH), BF16),
        ],
        compiler_params=_params(2),
        name="lru",
    )(h, g.reshape(1, D_MODEL), w_in, conv_w, conv_b.reshape(1, -1), gate_w, gate_b,
      lam.reshape(1, -1), w_out)


KV_T = 512
HEAD_GROUP = 8
GROUP_COLS = HEAD_GROUP * HEAD_DIM
AUG_DIM = 2 * HEAD_DIM
V_AUG_DIM = HEAD_DIM + 2 * SUBLANES


def _head_rms_norm(x, g):
    return x * lax.rsqrt(jnp.mean(x * x, axis=-1, keepdims=True) + EPS) * g


def _kv_kernel(h_ref, g_ref, wkv_ref, kg_ref, k_ref, vt_ref, km_ref):
    s = pl.program_id(1)
    t = KV_T
    xn = _rms_norm(h_ref[0], g_ref[...]).astype(BF16)
    n_groups = N_HEADS // HEAD_GROUP
    k_cols = N_HEADS * HEAD_DIM

    ones_rows = jnp.where(lax.broadcasted_iota(jnp.int32, (V_AUG_DIM - HEAD_DIM, t), 0) == 0,
                          1.0, 0.0).astype(BF16)
    row = lax.broadcasted_iota(jnp.int32, (t, LANES), 0)
    lane = lax.broadcasted_iota(jnp.int32, (t, LANES), 1)
    block_of_row = s * (t // MOBA_BLOCK) + row // MOBA_BLOCK
    onehot = jnp.where(lane == block_of_row, 1.0, 0.0).astype(BF16)

    for n in range(n_groups):
        k = _dot(xn, wkv_ref[:, n * GROUP_COLS:(n + 1) * GROUP_COLS])
        v = _dot(xn, wkv_ref[:, k_cols + n * GROUP_COLS:k_cols + (n + 1) * GROUP_COLS])
        v_t = v.T.astype(BF16)
        for hl in range(HEAD_GROUP):
            head = n * HEAD_GROUP + hl
            vt_ref[0, head * V_AUG_DIM:head * V_AUG_DIM + HEAD_DIM, :] = (
                v_t[hl * HEAD_DIM:(hl + 1) * HEAD_DIM])
            vt_ref[0, head * V_AUG_DIM + HEAD_DIM:(head + 1) * V_AUG_DIM, :] = ones_rows
        for hl in range(HEAD_GROUP):
            head = n * HEAD_GROUP + hl
            kh = _head_rms_norm(k[:, hl * HEAD_DIM:(hl + 1) * HEAD_DIM], kg_ref[...])
            k_ref[0, :, head * AUG_DIM:head * AUG_DIM + HEAD_DIM] = kh.astype(BF16)
            k_ref[0, :, head * AUG_DIM + HEAD_DIM:(head + 1) * AUG_DIM] = onehot
            for blk in range(t // MOBA_BLOCK):
                km_ref[0, blk, :, head * HEAD_DIM:(head + 1) * HEAD_DIM] = jnp.mean(
                    kh[blk * MOBA_BLOCK:(blk + 1) * MOBA_BLOCK], axis=0, keepdims=True)


def _shared_kv(h, g, w_kv, k_g):
    bsz, seq, _ = h.shape
    t = KV_T
    n_blocks = seq // MOBA_BLOCK
    return pl.pallas_call(
        _kv_kernel,
        grid=(bsz, seq // t),
        in_specs=[
            pl.BlockSpec((1, t, D_MODEL), lambda b, s: (b, s, 0)),
            _resident((1, D_MODEL)),
            _resident(w_kv.shape),
            _resident((1, HEAD_DIM)),
        ],
        out_specs=[
            pl.BlockSpec((1, t, N_HEADS * AUG_DIM), lambda b, s: (b, s, 0)),
            pl.BlockSpec((1, N_HEADS * V_AUG_DIM, t), lambda b, s: (b, 0, s)),
            pl.BlockSpec((1, t // MOBA_BLOCK, 1, N_HEADS * HEAD_DIM), lambda b, s: (b, s, 0, 0)),
        ],
        out_shape=[
            jax.ShapeDtypeStruct((bsz, seq, N_HEADS * AUG_DIM), BF16),
            jax.ShapeDtypeStruct((bsz, N_HEADS * V_AUG_DIM, seq), BF16),
            jax.ShapeDtypeStruct((bsz, n_blocks, 1, N_HEADS * HEAD_DIM), F32),
        ],
        compiler_params=_params(2),
        name="shared_kv",
    )(h, g.reshape(1, D_MODEL), w_kv, k_g.reshape(1, HEAD_DIM))


Q_T = 512
GATE_SLOTS = SUBLANES * 2


def _split_bf16(x):
    hi = x.astype(BF16)
    lo = (x - hi.astype(F32)).astype(BF16)
    return hi, lo


def _q_kernel(h_ref, g_ref, wqt_ref, qg_ref, km_ref, qt_ref):
    s = pl.program_id(1)
    xn = _rms_norm(h_ref[0], g_ref[...]).astype(BF16)
    projections = [_dot_nt(wqt_ref[n * GROUP_COLS:(n + 1) * GROUP_COLS, :], xn)
                   for n in range(N_HEADS // HEAD_GROUP)]
    for n, q_t in enumerate(projections):
        _q_group(s, q_t, qg_ref, km_ref[0, :, n * GROUP_COLS:(n + 1) * GROUP_COLS],
                 qt_ref.at[0, n * HEAD_GROUP * AUG_DIM:(n + 1) * HEAD_GROUP * AUG_DIM, :])


def _q_group(s, q_t, qg_ref, km, out_ref):
    t = Q_T
    qg = jnp.broadcast_to(qg_ref[...], (HEAD_DIM, t))
    heads = []
    for hl in range(HEAD_GROUP):
        x = q_t[hl * HEAD_DIM:(hl + 1) * HEAD_DIM]
        heads.append(x * lax.rsqrt(jnp.mean(x * x, axis=0, keepdims=True) + EPS) * qg)
    qn_t = jnp.concatenate(heads, axis=0)

    km_rows = jnp.concatenate([km] * HEAD_GROUP, axis=0)
    r_idx = lax.broadcasted_iota(jnp.int32, (HEAD_GROUP * GATE_SLOTS, GROUP_COLS), 0)
    c_idx = lax.broadcasted_iota(jnp.int32, (HEAD_GROUP * GATE_SLOTS, GROUP_COLS), 1)
    km_diag = jnp.where(r_idx // GATE_SLOTS == c_idx // HEAD_DIM, km_rows, 0.0)
    q_hi, q_lo = _split_bf16(qn_t)
    k_hi, k_lo = _split_bf16(km_diag)
    gate = _dot(k_hi, q_hi) + (_dot(k_lo, q_hi) + _dot(k_hi, q_lo))

    blk = lax.broadcasted_iota(jnp.int32, (GATE_SLOTS, t), 0)
    col = lax.broadcasted_iota(jnp.int32, (GATE_SLOTS, t), 1)
    own = s * (t // MOBA_BLOCK) + col // MOBA_BLOCK
    past = blk < own
    zeros_tail = jnp.zeros((AUG_DIM - HEAD_DIM - GATE_SLOTS, t), BF16)
    for hl in range(HEAD_GROUP):
        g = jnp.where(past, gate[hl * GATE_SLOTS:(hl + 1) * GATE_SLOTS], -jnp.inf)
        selected = blk < 0
        for _ in range(MOBA_TOPK):
            best = jnp.max(g, axis=0, keepdims=True)
            first = jnp.min(jnp.where(g == best, blk, GATE_SLOTS), axis=0, keepdims=True)
            pick = blk == first
            selected = selected | pick
            g = jnp.where(pick, -jnp.inf, g)
        attend = (selected & past) | (blk >= own)
        bias = jnp.where(attend, 0.0, MASK_VALUE)
        base = hl * AUG_DIM
        out_ref[base:base + HEAD_DIM, :] = (heads[hl] * QK_SCALE).astype(BF16)
        out_ref[base + HEAD_DIM:base + HEAD_DIM + GATE_SLOTS, :] = bias.astype(BF16)
        out_ref[base + HEAD_DIM + GATE_SLOTS:base + AUG_DIM, :] = zeros_tail


def _queries(h, g, w_q_t, q_g, k_mean):
    bsz, seq, _ = h.shape
    t = Q_T
    n_blocks = seq // MOBA_BLOCK
    assert n_blocks == GATE_SLOTS
    return pl.pallas_call(
        _q_kernel,
        grid=(bsz, seq // t),
        in_specs=[
            pl.BlockSpec((1, t, D_MODEL), lambda b, s: (b, s, 0)),
            _resident((1, D_MODEL)),
            _resident((N_HEADS * HEAD_DIM, D_MODEL)),
            _resident((HEAD_DIM, 1)),
            pl.BlockSpec((1, n_blocks, N_HEADS * HEAD_DIM), lambda b, s: (b, 0, 0)),
        ],
        out_specs=pl.BlockSpec((1, N_HEADS * AUG_DIM, t), lambda b, s: (b, 0, s)),
        out_shape=jax.ShapeDtypeStruct((bsz, N_HEADS * AUG_DIM, seq), BF16),
        compiler_params=_params(2),
        name="queries",
    )(h, g.reshape(1, D_MODEL), w_q_t, q_g.reshape(HEAD_DIM, 1),
      k_mean.reshape(bsz, n_blocks, N_HEADS * HEAD_DIM))


def _attn_kernel(qt_ref, k_ref, vt_ref, o_ref, m_ref, acc_ref):
    blk = MOBA_BLOCK
    n_blocks = qt_ref.shape[2] // blk
    k_pos = lax.broadcasted_iota(jnp.int32, (blk, blk), 0)
    q_pos = lax.broadcasted_iota(jnp.int32, (blk, blk), 1)
    causal = k_pos <= q_pos

    def rows(j):
        return slice(j * blk, (j + 1) * blk)

    own = [_dot(k_ref[0, rows(i), :], qt_ref[0, :, rows(i)]) for i in range(n_blocks)]
    for i in range(n_blocks):
        s = jnp.where(causal, own[i], MASK_VALUE)
        m = jnp.max(s, axis=0, keepdims=True)
        p = jnp.exp2(s - m)
        m_ref[:, rows(i)] = m
        acc_ref[:, rows(i)] = _dot(vt_ref[0, :, rows(i)], p.astype(BF16))

    def past_scores(j):
        return _dot(k_ref[0, rows(j), :], qt_ref[0, :, (j + 1) * blk:])

    s_next = past_scores(0)
    for j in range(n_blocks - 1):
        later = slice((j + 1) * blk, n_blocks * blk)
        s = s_next
        if j + 2 < n_blocks:
            s_next = past_scores(j + 1)
        m_old = m_ref[:, later]
        m_new = jnp.maximum(m_old, jnp.max(s, axis=0, keepdims=True))
        alpha = jnp.exp2(m_old - m_new)
        p = jnp.exp2(s - m_new)
        m_ref[:, later] = m_new
        acc_ref[:, later] = alpha * acc_ref[:, later] + _dot(vt_ref[0, :, rows(j)], p.astype(BF16))

    o = acc_ref[:HEAD_DIM, :] * (1.0 / acc_ref[HEAD_DIM:HEAD_DIM + 1, :])
    o_ref[0] = o.T.astype(BF16)


def _attention(q_aug_t, k_aug, v_aug_t):
    bsz, _, seq = v_aug_t.shape
    return pl.pallas_call(
        _attn_kernel,
        grid=(bsz, N_HEADS),
        in_specs=[
            pl.BlockSpec((1, AUG_DIM, seq), lambda b, h: (b, h, 0)),
            pl.BlockSpec((1, seq, AUG_DIM), lambda b, h: (b, 0, h)),
            pl.BlockSpec((1, V_AUG_DIM, seq), lambda b, h: (b, h, 0)),
        ],
        out_specs=pl.BlockSpec((1, seq, HEAD_DIM), lambda b, h: (b, 0, h)),
        out_shape=jax.ShapeDtypeStruct((bsz, seq, N_HEADS * HEAD_DIM), BF16),
        scratch_shapes=[
            pltpu.VMEM((1, seq), F32),
            pltpu.VMEM((V_AUG_DIM, seq), F32),
        ],
        compiler_params=_params(2),
        name="moba_attention",
    )(q_aug_t, k_aug, v_aug_t)


PROJ_TM = 512


def _out_proj_kernel(o_ref, w_ref, x_ref, y_ref):
    y_ref[...] = x_ref[...] + _dot(o_ref[...], w_ref[...])


def _out_proj(o2d, w_o, x2d):
    m, k = o2d.shape
    n = w_o.shape[1]
    return pl.pallas_call(
        _out_proj_kernel,
        grid=(m // PROJ_TM,),
        in_specs=[
            pl.BlockSpec((PROJ_TM, k), lambda i: (i, 0)),
            pl.BlockSpec((k, n), lambda i: (0, 0), pipeline_mode=pl.Buffered(1)),
            pl.BlockSpec((PROJ_TM, n), lambda i: (i, 0)),
        ],
        out_specs=pl.BlockSpec((PROJ_TM, n), lambda i: (i, 0)),
        out_shape=jax.ShapeDtypeStruct((m, n), F32),
        compiler_params=_params(1),
        name="out_proj",
    )(o2d, w_o, x2d)


def kernel(x, norm_g, ffn_w_in, ffn_w_out, lru_w_in, lru_conv_w, lru_conv_b, lru_gate_w, lru_gate_b, lru_lambda, lru_w_out, kv_norm_g, w_kv, k_norm_g, attn_w_q, q_norm_g, attn_w_o):
    bsz, seq, d = x.shape
    rows = bsz * seq

    def ffn(h, layer, which):
        return _ffn(h.reshape(rows, d), norm_g[layer, 2 * which], ffn_w_in, ffn_w_out,
                    layer, which).reshape(bsz, seq, d)

    h = ffn(x, 0, 0)
    h = _lru(h, norm_g[0, 1], lru_w_in[0].astype(BF16), lru_conv_w[0], lru_conv_b[0],
             lru_gate_w[0].astype(BF16), lru_gate_b[0], lru_lambda[0], lru_w_out[0].astype(BF16))
    h = ffn(h, 0, 1)
    k_aug, v_t, k_mean = _shared_kv(h, kv_norm_g, w_kv.astype(BF16), k_norm_g)

    h = ffn(h, 1, 0)
    q_aug_t = _queries(h, norm_g[1, 1], attn_w_q[0].T.astype(BF16), q_norm_g[0], k_mean)
    o = _attention(q_aug_t, k_aug, v_t)
    h = _out_proj(o.reshape(rows, d), attn_w_o[0].astype(BF16), h.reshape(rows, d)).reshape(bsz, seq, d)
    h = ffn(h, 1, 1)
    return h
```
